```python
import jax, jax.numpy as jnp
from jax import lax
import numpy as np

D_MODEL = 2048
BATCH = 4
SEQ = 2048
DEPTH = 1

CHUNK = 64
MIX_WIDTH = D_MODEL
ATT_WIDTH = MIX_WIDTH // 2
HG_WIDTH = MIX_WIDTH - ATT_WIDTH
ATT_HEAD_DIM = 128
ATT_HEADS = ATT_WIDTH // ATT_HEAD_DIM
ATT_LEFT_CHUNKS = 8
ATT_BAND = ATT_LEFT_CHUNKS + 1
MAX_REL = 128
HG_EXPAND = 128
HG_HEADS = HG_WIDTH // HG_EXPAND
HG_KDIM = HG_EXPAND
HG_FDIM = HG_HEADS * HG_KDIM
HG_VDIM = HG_WIDTH // HG_HEADS
IN_SPLITS = (ATT_WIDTH, ATT_WIDTH, ATT_WIDTH, ATT_WIDTH, HG_FDIM, HG_FDIM, HG_WIDTH, HG_WIDTH)
IN_COLS = sum(IN_SPLITS)
EPS = 1e-6

kernel_name = "hymba_chunkattn_hgrn2_adaln_block"


def _rms(x, g):
    xf = x.astype(jnp.float32)
    y = xf * lax.rsqrt(jnp.mean(xf * xf, axis=-1, keepdims=True) + EPS)
    return (y * g.astype(jnp.float32)).astype(x.dtype)


def _chunk_attention(q, k, v, q_g, k_g, rel_bias):
    B, T, H, dh = q.shape
    N = T // CHUNK
    q = _rms(q, q_g).reshape(B, N, CHUNK, H, dh)
    k = _rms(k, k_g).reshape(B, N, CHUNK, H, dh)
    v = v.reshape(B, N, CHUNK, H, dh)
    pad = ((0, 0), (ATT_LEFT_CHUNKS, 0), (0, 0), (0, 0), (0, 0))
    kp = jnp.pad(k, pad)
    vp = jnp.pad(v, pad)
    band_idx = jnp.arange(N)[:, None] + jnp.arange(ATT_BAND)[None, :]
    kb = kp[:, band_idx].reshape(B, N, ATT_BAND * CHUNK, H, dh)
    vb = vp[:, band_idx].reshape(B, N, ATT_BAND * CHUNK, H, dh)
    scores = jnp.einsum('bnqhd,bnkhd->bnhqk', q, kb).astype(jnp.float32) * (dh ** -0.5)
    qi = jnp.arange(CHUNK)
    kj = jnp.arange(ATT_BAND * CHUNK)
    dist = ATT_LEFT_CHUNKS * CHUNK + qi[:, None] - kj[None, :]
    bias_idx = jnp.clip(dist, -MAX_REL, MAX_REL) + MAX_REL
    bias = rel_bias.astype(jnp.float32)[:, bias_idx]
    valid = (jnp.arange(N)[:, None] + kj[None, :] // CHUNK) >= ATT_LEFT_CHUNKS
    scores = jnp.where(valid[None, :, None, None, :], scores + bias[None, None], -jnp.inf)
    p = jax.nn.softmax(scores, axis=-1).astype(v.dtype)
    out = jnp.einsum('bnhqk,bnkhd->bnqhd', p, vb)
    return out.reshape(B, T, H * dh)


def _hgrn2(q_raw, f_raw, i, lb, o_g):
    B, T, _ = q_raw.shape
    N = T // CHUNK
    out_dtype = i.dtype
    f32 = jnp.float32
    q = jax.nn.silu(q_raw.astype(f32))
    fr = f_raw.astype(f32)
    lb = lb.astype(f32)
    logf = jnp.log(lb + (1.0 - lb) * jax.nn.sigmoid(fr))
    k = (1.0 - lb) * jax.nn.sigmoid(-fr)
    v = i.astype(f32)

    def to_chunks(a, d):
        return a.reshape(B, N, CHUNK, HG_HEADS, d).transpose(1, 0, 3, 2, 4)

    qc, kc, lc = to_chunks(q, HG_KDIM), to_chunks(k, HG_KDIM), to_chunks(logf, HG_KDIM)
    vc = to_chunks(v, HG_VDIM)
    tril = jnp.arange(CHUNK)[:, None] >= jnp.arange(CHUNK)[None, :]

    def step(S, xs):
        qn, kn, vn, ln = xs
        b = jnp.cumsum(ln, axis=2)
        b_last = b[:, :, -1]
        inter = jnp.einsum('bhcd,bhde->bhce', qn * jnp.exp(b), S)
        diff = b[:, :, :, None, :] - b[:, :, None, :, :]
        decay = jnp.exp(jnp.where(tril[None, None, :, :, None], diff, -jnp.inf))
        A = jnp.einsum('bhtd,bhtsd,bhsd->bhts', qn, decay, kn)
        intra = jnp.einsum('bhts,bhse->bhte', A, vn)
        k_dec = kn * jnp.exp(b_last[:, :, None, :] - b)
        S_new = jnp.exp(b_last)[..., None] * S + jnp.einsum('bhsd,bhse->bhde', k_dec, vn)
        return S_new, inter + intra

    S0 = jnp.zeros((B, HG_HEADS, HG_KDIM, HG_VDIM), f32)
    _, o = lax.scan(step, S0, (qc, kc, vc, lc))
    o = o.transpose(1, 0, 3, 2, 4).reshape(B, T, HG_HEADS, HG_VDIM)
    o = _rms(o, o_g)
    return o.reshape(B, T, HG_WIDTH).astype(out_dtype)


def setup_inputs(seed: int = 0) -> dict:
    key = jax.random.key(seed)
    ks = jax.random.split(key, 16)
    D = D_MODEL
    f32 = jnp.float32
    nrm = lambda k, s: jax.random.normal(k, s, f32)
    return {
        "x": nrm(ks[0], (BATCH, SEQ, D)),
        "c": nrm(ks[1], (BATCH, D)),
        "norm_g": 1.0 + 0.1 * nrm(ks[2], (DEPTH, D)),
        "w_ada": nrm(ks[3], (DEPTH, D, 3 * D)) * D ** -0.5,
        "b_ada": 0.01 * nrm(ks[4], (DEPTH, 3 * D)),
        "w_in": nrm(ks[5], (DEPTH, D, IN_COLS)) * D ** -0.5,
        "q_norm_g": 1.0 + 0.1 * nrm(ks[6], (DEPTH, ATT_HEAD_DIM)),
        "k_norm_g": 1.0 + 0.1 * nrm(ks[7], (DEPTH, ATT_HEAD_DIM)),
        "rel_bias": 0.5 * nrm(ks[8], (DEPTH, ATT_HEADS, 2 * MAX_REL + 1)),
        "lower_bounds": nrm(ks[9], (DEPTH + 1, HG_FDIM)),
        "hg_norm_g": 1.0 + 0.1 * nrm(ks[10], (DEPTH, HG_VDIM)),
        "w_out": nrm(ks[11], (DEPTH, MIX_WIDTH, D)) * MIX_WIDTH ** -0.5,
    }


def reference(x, c, norm_g, w_ada, b_ada, w_in, q_norm_g, k_norm_g, rel_bias,
              lower_bounds, hg_norm_g, w_out):
    B, T, D = x.shape
    lbs = jnp.cumsum(jax.nn.softmax(lower_bounds.astype(jnp.float32), axis=0), axis=0)
    offsets = np.cumsum(IN_SPLITS)[:-1].tolist()
    c_act = jax.nn.silu(c)
    for l in range(DEPTH):
        mod = c_act @ w_ada[l] + b_ada[l]
        shift, scale, gate = jnp.split(mod, 3, axis=-1)
        h = _rms(x, norm_g[l]) * (1.0 + scale[:, None, :]) + shift[:, None, :]
        proj = h @ w_in[l]
        a_q, a_k, a_v, a_z, g_q, g_f, g_i, g_z = jnp.split(proj, offsets, axis=-1)
        hs = (B, T, ATT_HEADS, ATT_HEAD_DIM)
        att = _chunk_attention(a_q.reshape(hs), a_k.reshape(hs), a_v.reshape(hs),
                               q_norm_g[l], k_norm_g[l], rel_bias[l])
        att = att * jax.nn.silu(a_z)
        hg = _hgrn2(g_q, g_f, g_i, lbs[l], hg_norm_g[l]) * jax.nn.silu(g_z)
        y = jnp.concatenate([att, hg], axis=-1) @ w_out[l]
        x = x + gate[:, None, :] * y
    return x
```

```python
import functools

import jax
import jax.numpy as jnp
from jax import lax
from jax.experimental import pallas as pl
from jax.experimental.pallas import tpu as pltpu

F32 = jnp.float32
BF16 = jnp.bfloat16

CHUNK = 64
HEAD_DIM = 128
N_HEADS = 8
GROUP = N_HEADS * HEAD_DIM
N_GROUPS = 8
LEFT_CHUNKS = 8
MAX_REL = 128
EPS = 1e-6
EXP_CLAMP = 80.0

VMEM_LIMIT = 56 * 1024 * 1024

ADA_TN = 768
PROJ_TM = 512
ATT_RQ = 256
ATT_WK = ATT_RQ + LEFT_CHUNKS * CHUNK
HG_RB = 256
OUT_TM = 512
EPI_RB = 128


def _silu(v):
    return v * jax.nn.sigmoid(v)


def _ada_kernel(c_ref, w_ref, b_ref, o_ref):
    ca = _silu(c_ref[...])
    o_ref[...] = jnp.dot(ca, w_ref[...], preferred_element_type=F32,
                         precision=lax.Precision.HIGHEST) + b_ref[...]


def _ada_call(c_pad, w_ada, b_ada):
    rows, d = c_pad.shape
    n = w_ada.shape[-1]
    return pl.pallas_call(
        _ada_kernel,
        grid=(n // ADA_TN,),
        in_specs=[
            pl.BlockSpec((rows, d), lambda j: (0, 0)),
            pl.BlockSpec((None, d, ADA_TN), lambda j: (0, 0, j)),
            pl.BlockSpec((1, ADA_TN), lambda j: (0, j)),
        ],
        out_specs=pl.BlockSpec((rows, ADA_TN), lambda j: (0, j)),
        out_shape=jax.ShapeDtypeStruct((rows, n), F32),
        compiler_params=pltpu.CompilerParams(
            dimension_semantics=("arbitrary",), vmem_limit_bytes=VMEM_LIMIT),
        name="ada",
    )(c_pad, w_ada, b_ada)


def _proj_kernel(x_ref, shift_ref, scale_ref, ng_ref, w_ref, qg_ref, kg_ref, lb_ref,
                 out_ref, logf_ref, h_scr, acc_scr):
    j = pl.program_id(1)
    tm = x_ref.shape[0]
    n_rb = tm // EPI_RB

    def rows_of(r):
        return pl.ds(pl.multiple_of(r * EPI_RB, EPI_RB), EPI_RB)

    def for_row_blocks(fn):
        def body(r, carry):
            fn(rows_of(r))
            return carry
        lax.fori_loop(0, n_rb, body, 0)

    @pl.when(j == 0)
    def _():
        g = ng_ref[...]
        sc = 1.0 + scale_ref[...]
        sh = shift_ref[...]

        def norm_rows(rows):
            xv = x_ref[rows, :]
            ms = jnp.mean(xv * xv, axis=-1, keepdims=True)
            y = xv * lax.rsqrt(ms + EPS) * g
            h_scr[rows, :] = (y * sc + sh).astype(BF16)
        for_row_blocks(norm_rows)

    acc_scr[...] = jnp.dot(h_scr[...], w_ref[...], preferred_element_type=F32)

    def head_rms(rows, gain, scale):
        for h in range(N_HEADS):
            hs = slice(h * HEAD_DIM, (h + 1) * HEAD_DIM)
            a = acc_scr[rows, hs]
            ms = jnp.mean(a * a, axis=-1, keepdims=True)
            y = a * lax.rsqrt(ms + EPS) * gain
            if scale != 1.0:
                y = y * scale
            out_ref[rows, hs] = y.astype(BF16)

    @pl.when(j == 0)
    def _():
        for_row_blocks(lambda rows: head_rms(rows, qg_ref[...], HEAD_DIM ** -0.5))

    @pl.when(j == 1)
    def _():
        for_row_blocks(lambda rows: head_rms(rows, kg_ref[...], 1.0))

    @pl.when((j == 2) | (j == 6))
    def _():
        def cast_rows(rows):
            out_ref[rows, :] = acc_scr[rows, :].astype(BF16)
        for_row_blocks(cast_rows)

    @pl.when((j == 3) | (j == 4) | (j == 7))
    def _():
        def silu_rows(rows):
            out_ref[rows, :] = _silu(acc_scr[rows, :]).astype(BF16)
        for_row_blocks(silu_rows)

    @pl.when(j == 5)
    def _():
        lw = lb_ref[...]
        e = jnp.exp(lw - jnp.max(lw, axis=0, keepdims=True))
        lb = e[0:1, :] / jnp.sum(e, axis=0, keepdims=True)
        one_m_lb = 1.0 - lb

        def gate_rows(rows):
            fr = acc_scr[rows, :]
            logf_ref[rows, :] = jnp.log(lb + one_m_lb * jax.nn.sigmoid(fr))
            out_ref[rows, :] = (one_m_lb * jax.nn.sigmoid(-fr)).astype(BF16)
        for_row_blocks(gate_rows)


def _proj_call(x2, mod3, norm_g, w_in_bf, q_g, k_g, lower_bounds, seq_len):
    m, d = x2.shape
    tiles_per_batch = seq_len // PROJ_TM
    grid = (m // PROJ_TM, N_GROUPS)
    return pl.pallas_call(
        _proj_kernel,
        grid=grid,
        in_specs=[
            pl.BlockSpec((PROJ_TM, d), lambda i, j: (i, 0)),
            pl.BlockSpec((None, 1, d), lambda i, j: (i // tiles_per_batch, 0, 0)),
            pl.BlockSpec((None, 1, d), lambda i, j: (i // tiles_per_batch, 0, 1)),
            pl.BlockSpec((1, d), lambda i, j: (0, 0)),
            pl.BlockSpec((d, GROUP), lambda i, j: (0, j)),
            pl.BlockSpec((1, HEAD_DIM), lambda i, j: (0, 0)),
            pl.BlockSpec((1, HEAD_DIM), lambda i, j: (0, 0)),
            pl.BlockSpec(lower_bounds.shape, lambda i, j: (0, 0)),
        ],
        out_specs=[
            pl.BlockSpec((PROJ_TM, GROUP), lambda i, j: (i, j)),
            pl.BlockSpec((PROJ_TM, GROUP), lambda i, j: (i, 0)),
        ],
        out_shape=[
            jax.ShapeDtypeStruct((m, N_GROUPS * GROUP), BF16),
            jax.ShapeDtypeStruct((m, GROUP), F32),
        ],
        scratch_shapes=[
            pltpu.VMEM((PROJ_TM, d), BF16),
            pltpu.VMEM((PROJ_TM, GROUP), F32),
        ],
        compiler_params=pltpu.CompilerParams(
            dimension_semantics=("arbitrary", "arbitrary"), vmem_limit_bytes=VMEM_LIMIT),
        name="proj",
    )(x2, mod3, mod3, norm_g, w_in_bf, q_g, k_g, lower_bounds)


def _build_bias(relb_ref, bias_scr):
    band_w = 4 * CHUNK
    r = lax.broadcasted_iota(jnp.int32, (CHUNK, band_w), 0)
    c = lax.broadcasted_iota(jnp.int32, (CHUNK, band_w), 1)
    dist = 3 * CHUNK + r - c
    n_var = CHUNK - 1 + MAX_REL
    far_w = (LEFT_CHUNKS - 3) * CHUNK
    for h in range(N_HEADS):
        far = relb_ref[h, 2 * MAX_REL]

        def body(d, acc, h=h):
            return jnp.where(dist == d - (CHUNK - 1), relb_ref[h, d + MAX_REL - (CHUNK - 1)], acc)
        near = lax.fori_loop(0, n_var, body, jnp.full((CHUNK, band_w), far, F32))
        for cq in range(ATT_RQ // CHUNK):
            rows = slice(cq * CHUNK, (cq + 1) * CHUNK)
            lo = cq * CHUNK
            bias_scr[h, rows, :] = jnp.full((CHUNK, ATT_WK), -jnp.inf, F32)
            bias_scr[h, rows, lo:lo + far_w] = jnp.full((CHUNK, far_w), far, F32)
            bias_scr[h, rows, lo + far_w:lo + far_w + band_w] = near


def _att_kernel(relb_ref, q_ref, k0_ref, k1_ref, k2_ref, v0_ref, v1_ref, v2_ref, z_ref,
                o_ref, bias_scr):
    b = pl.program_id(0)
    i = pl.program_id(1)

    @pl.when((b == 0) & (i == 0))
    def _():
        _build_bias(relb_ref, bias_scr)

    kpos = lax.broadcasted_iota(jnp.int32, (ATT_RQ, ATT_WK), 1)
    valid = kpos >= (LEFT_CHUNKS * CHUNK - ATT_RQ * i)
    nt = (((1,), (1,)), ((), ()))
    for h in range(N_HEADS):
        hs = slice(h * HEAD_DIM, (h + 1) * HEAD_DIM)
        q = q_ref[:, hs]
        s = jnp.concatenate(
            [lax.dot_general(q, kr[:, hs], nt, preferred_element_type=F32)
             for kr in (k0_ref, k1_ref, k2_ref)], axis=1)
        s = jnp.where(valid, s + bias_scr[h], -jnp.inf)
        m = jnp.max(s, axis=-1, keepdims=True)
        p = jnp.exp(s - m)
        l = jnp.sum(p, axis=-1, keepdims=True)
        pb = p.astype(BF16)
        o = jnp.dot(pb[:, 0:ATT_RQ], v0_ref[:, hs], preferred_element_type=F32)
        o += jnp.dot(pb[:, ATT_RQ:2 * ATT_RQ], v1_ref[:, hs], preferred_element_type=F32)
        o += jnp.dot(pb[:, 2 * ATT_RQ:3 * ATT_RQ], v2_ref[:, hs], preferred_element_type=F32)
        o = (o / l) * z_ref[:, hs].astype(F32)
        o_ref[:, hs] = o.astype(BF16)


def _att_call(rel_bias, proj, batch, seq_len):
    nq = seq_len // ATT_RQ
    m = proj.shape[0]

    def kv_spec(back, col):
        return pl.BlockSpec((ATT_RQ, GROUP),
                            lambda b, i: (b * nq + jnp.maximum(i - back, 0), col))
    return pl.pallas_call(
        _att_kernel,
        grid=(batch, nq),
        in_specs=[
            pl.BlockSpec(memory_space=pltpu.SMEM),
            pl.BlockSpec((ATT_RQ, GROUP), lambda b, i: (b * nq + i, 0)),
            kv_spec(2, 1), kv_spec(1, 1), kv_spec(0, 1),
            kv_spec(2, 2), kv_spec(1, 2), kv_spec(0, 2),
            pl.BlockSpec((ATT_RQ, GROUP), lambda b, i: (b * nq + i, 3)),
        ],
        out_specs=pl.BlockSpec((ATT_RQ, GROUP), lambda b, i: (b * nq + i, 0)),
        out_shape=jax.ShapeDtypeStruct((m, GROUP), BF16),
        scratch_shapes=[pltpu.VMEM((N_HEADS, ATT_RQ, ATT_WK), F32)],
        compiler_params=pltpu.CompilerParams(
            dimension_semantics=("arbitrary", "arbitrary"), vmem_limit_bytes=VMEM_LIMIT),
        name="att",
    )(rel_bias, proj, proj, proj, proj, proj, proj, proj, proj)


def _hg_kernel(q_ref, k_ref, v_ref, z_ref, logf_ref, og_ref, o_ref,
               st_scr, intra_scr, kf_scr, vf_scr, bf_scr):
    t = pl.program_id(1)

    @pl.when(t == 0)
    def _():
        st_scr[...] = jnp.zeros_like(st_scr)

    row_i = lax.broadcasted_iota(jnp.int32, (CHUNK, CHUNK), 0)
    col_i = lax.broadcasted_iota(jnp.int32, (CHUNK, CHUNK), 1)
    tril = row_i >= col_i
    tri_bf = jnp.where(tril, 1.0, 0.0).astype(BF16)
    mid = CHUNK // 2 - 1
    nt = (((1,), (1,)), ((), ()))
    tn = (((0,), (0,)), ((), ()))
    og = og_ref[...]

    def chunk_body(ci, carry):
        rows = pl.ds(pl.multiple_of(ci * CHUNK, CHUNK), CHUNK)
        lf = logf_ref[rows, :]
        hi = lf.astype(BF16)
        r1 = lf - hi.astype(F32)
        md = r1.astype(BF16)
        lo = (r1 - md.astype(F32)).astype(BF16)
        bcum = (jnp.dot(tri_bf, lo, preferred_element_type=F32)
                + jnp.dot(tri_bf, md, preferred_element_type=F32)
                + jnp.dot(tri_bf, hi, preferred_element_type=F32))
        b_first = bcum[0:1, :]
        b_mid = bcum[mid:mid + 1, :]
        b_last = bcum[CHUNK - 1:CHUNK, :]
        steep = jnp.max(jnp.maximum(b_first - b_mid, b_mid - b_last)) > EXP_CLAMP

        inters = []
        for h in range(N_HEADS):
            hs = slice(h * HEAD_DIM, (h + 1) * HEAD_DIM)
            qt = q_ref[rows, hs].astype(F32)
            kh = k_ref[rows, hs].astype(F32)
            vh = v_ref[rows, hs]
            bh = bcum[:, hs]
            bm = b_mid[:, hs]
            bl = b_last[:, hs]
            st = st_scr[h]
            qb = (qt * jnp.exp(bh)).astype(BF16)
            inters.append(lax.dot_general(qb, st.astype(BF16), nt, preferred_element_type=F32))
            qp = (qt * jnp.exp(jnp.minimum(bh - bm, EXP_CLAMP))).astype(BF16)
            kp = (kh * jnp.exp(jnp.minimum(bm - bh, EXP_CLAMP))).astype(BF16)
            a = lax.dot_general(qp, kp, nt, preferred_element_type=F32)
            a = jnp.where(tril, a, 0.0).astype(BF16)
            intra_scr[:, hs] = jnp.dot(a, vh, preferred_element_type=F32)
            kdec = (kh * jnp.exp(bl - bh)).astype(BF16)
            upd = lax.dot_general(vh, kdec, tn, preferred_element_type=F32)
            st_scr[h] = st * jnp.exp(bl) + upd

        @pl.when(steep)
        def _():
            kf_scr[...] = k_ref[rows, :].astype(F32)
            vf_scr[...] = v_ref[rows, :].astype(F32)
            bf_scr[...] = bcum
            intra_scr[...] = jnp.zeros_like(intra_scr)
            t_idx = lax.broadcasted_iota(jnp.int32, (CHUNK, HEAD_DIM), 0)

            def pair_body(s, c2):
                k_all = kf_scr[pl.ds(s, 1), :]
                v_all = vf_scr[pl.ds(s, 1), :]
                b_all = bf_scr[pl.ds(s, 1), :]
                for h in range(N_HEADS):
                    hs = slice(h * HEAD_DIM, (h + 1) * HEAD_DIM)
                    w = jnp.exp(jnp.minimum(bf_scr[:, hs] - b_all[:, hs], 0.0))
                    p = jnp.where(t_idx >= s, q_ref[rows, hs].astype(F32) * k_all[:, hs] * w, 0.0)
                    col = jnp.sum(p, axis=-1, keepdims=True)
                    intra_scr[:, hs] += col * v_all[:, hs]
                return c2
            lax.fori_loop(0, CHUNK, pair_body, 0)

        for h in range(N_HEADS):
            hs = slice(h * HEAD_DIM, (h + 1) * HEAD_DIM)
            o = inters[h] + intra_scr[:, hs]
            ms = jnp.mean(o * o, axis=-1, keepdims=True)
            y = o * lax.rsqrt(ms + EPS) * og
            o_ref[rows, hs] = (y * z_ref[rows, hs].astype(F32)).astype(BF16)
        return carry

    lax.fori_loop(0, HG_RB // CHUNK, chunk_body, 0)


def _hg_call(proj, logf, o_g, batch, seq_len):
    nt_ = seq_len // HG_RB
    m = proj.shape[0]

    def col_spec(col):
        return pl.BlockSpec((HG_RB, GROUP), lambda b, t: (b * nt_ + t, col))
    return pl.pallas_call(
        _hg_kernel,
        grid=(batch, nt_),
        in_specs=[
            col_spec(4), col_spec(5), col_spec(6), col_spec(7),
            pl.BlockSpec((HG_RB, GROUP), lambda b, t: (b * nt_ + t, 0)),
            pl.BlockSpec((1, HEAD_DIM), lambda b, t: (0, 0)),
        ],
        out_specs=pl.BlockSpec((HG_RB, GROUP), lambda b, t: (b * nt_ + t, 0)),
        out_shape=jax.ShapeDtypeStruct((m, GROUP), BF16),
        scratch_shapes=[
            pltpu.VMEM((N_HEADS, HEAD_DIM, HEAD_DIM), F32),
            pltpu.VMEM((CHUNK, GROUP), F32),
            pltpu.VMEM((CHUNK, GROUP), F32),
            pltpu.VMEM((CHUNK, GROUP), F32),
            pltpu.VMEM((CHUNK, GROUP), F32),
        ],
        compiler_params=pltpu.CompilerParams(
            dimension_semantics=("arbitrary", "arbitrary"), vmem_limit_bytes=VMEM_LIMIT),
        name="hgrn",
    )(proj, proj, proj, proj, logf, o_g)


def _out_kernel(att_ref, hg_ref, w_ref, x_ref, gate_ref, o_ref):
    y = jnp.dot(att_ref[...], w_ref[0:GROUP, :], preferred_element_type=F32)
    y += jnp.dot(hg_ref[...], w_ref[GROUP:2 * GROUP, :], preferred_element_type=F32)
    o_ref[...] = x_ref[...] + gate_ref[...] * y


def _out_call(att, hg, w_out_bf, x2, mod3, seq_len):
    m, d = x2.shape
    tiles_per_batch = seq_len // OUT_TM
    return pl.pallas_call(
        _out_kernel,
        grid=(m // OUT_TM,),
        in_specs=[
            pl.BlockSpec((OUT_TM, GROUP), lambda i: (i, 0)),
            pl.BlockSpec((OUT_TM, GROUP), lambda i: (i, 0)),
            pl.BlockSpec(w_out_bf.shape, lambda i: (0, 0)),
            pl.BlockSpec((OUT_TM, d), lambda i: (i, 0)),
            pl.BlockSpec((None, 1, d), lambda i: (i // tiles_per_batch, 0, 2)),
        ],
        out_specs=pl.BlockSpec((OUT_TM, d), lambda i: (i, 0)),
        out_shape=jax.ShapeDtypeStruct((m, d), F32),
        compiler_params=pltpu.CompilerParams(
            dimension_semantics=("arbitrary",), vmem_limit_bytes=VMEM_LIMIT),
        name="outproj",
    )(att, hg, w_out_bf, x2, mod3)


def kernel(x, c, norm_g, w_ada, b_ada, w_in, q_norm_g, k_norm_g, rel_bias,
           lower_bounds, hg_norm_g, w_out):
    batch, seq_len, d = x.shape
    assert w_in.shape == (1, d, N_GROUPS * GROUP) and w_out.shape == (1, 2 * GROUP, d)
    assert seq_len % PROJ_TM == 0 and seq_len % ATT_RQ == 0 and seq_len % HG_RB == 0
    assert rel_bias.shape == (1, N_HEADS, 2 * MAX_REL + 1)

    x2 = x.reshape(batch * seq_len, d)
    c_pad = jnp.pad(c, ((0, 8 - batch), (0, 0)))
    mod = _ada_call(c_pad, w_ada, b_ada)
    mod3 = mod[:batch].reshape(batch, 1, 3 * d)

    proj, logf = _proj_call(x2, mod3, norm_g, w_in[0].astype(BF16), q_norm_g, k_norm_g,
                            lower_bounds, seq_len)
    att = _att_call(rel_bias[0], proj, batch, seq_len)
    hg = _hg_call(proj, logf, hg_norm_g, batch, seq_len)
    out = _out_call(att, hg, w_out[0].astype(BF16), x2, mod3, seq_len)
    return out.reshape(batch, seq_len, d)
```

```python
import jax
import jax.numpy as jnp
from jax import lax
from jax.experimental import pallas as pl
from jax.experimental.pallas import tpu as pltpu

F32 = jnp.float32
BF16 = jnp.bfloat16

CHUNK = 64
HEAD_DIM = 128
N_HEADS = 8
GROUP = N_HEADS * HEAD_DIM
N_GROUPS = 8
LEFT_CHUNKS = 8
MAX_REL = 128
EPS = 1e-6
EXP_CLAMP = 80.0

VMEM_LIMIT = 56 * 1024 * 1024

ADA_TN = 768
PROJ_TM = 1024
NORM_RB = 128
ATT_RQ = 256
ATT_WK = ATT_RQ + LEFT_CHUNKS * CHUNK
HG_RB = 256
OUT_TM = 512

QZ_W = 2 * HEAD_DIM
HGP_W = 4 * HEAD_DIM


def _silu(v):
    return v * jax.nn.sigmoid(v)


def _lanes(piece, width=HEAD_DIM):
    return slice(piece * width, (piece + 1) * width)


def _ada_kernel(c_ref, w_ref, b_ref, o_ref):
    ca = _silu(c_ref[...])
    o_ref[...] = jnp.dot(ca, w_ref[...], preferred_element_type=F32,
                         precision=lax.Precision.HIGHEST) + b_ref[...]


def _ada_call(c_pad, w_ada, b_ada):
    rows, d = c_pad.shape
    n = w_ada.shape[-1]
    return pl.pallas_call(
        _ada_kernel,
        grid=(n // ADA_TN,),
        in_specs=[
            pl.BlockSpec((rows, d), lambda j: (0, 0)),
            pl.BlockSpec((None, d, ADA_TN), lambda j: (0, 0, j)),
            pl.BlockSpec((1, ADA_TN), lambda j: (0, j)),
        ],
        out_specs=pl.BlockSpec((rows, ADA_TN), lambda j: (0, j)),
        out_shape=jax.ShapeDtypeStruct((rows, n), F32),
        compiler_params=pltpu.CompilerParams(
            dimension_semantics=("arbitrary",), vmem_limit_bytes=VMEM_LIMIT),
        name="ada",
    )(c_pad, w_ada, b_ada)


def _proj_kernel(x_ref, shift_ref, scale_ref, ng_ref,
                 w0, w1, w2, w3, w4, w5, w6, w7,
                 qg_ref, kg_ref, lb_ref,
                 qz_ref, kv_ref, hgp_ref, logf_ref, h_scr, w_scr):
    j = pl.program_id(1)
    tm = x_ref.shape[0]

    @pl.when(j == 0)
    def _():
        g = ng_ref[...]
        sc = 1.0 + scale_ref[...]
        sh = shift_ref[...]

        def body(r, carry):
            rows = pl.ds(pl.multiple_of(r * NORM_RB, NORM_RB), NORM_RB)
            xv = x_ref[rows, :]
            ms = jnp.mean(xv * xv, axis=-1, keepdims=True)
            y = xv * lax.rsqrt(ms + EPS) * g
            h_scr[rows, :] = (y * sc + sh).astype(BF16)
            return carry
        lax.fori_loop(0, tm // NORM_RB, body, 0)

    for g, w_ref in enumerate((w0, w1, w2, w3, w4, w5, w6, w7)):
        w_scr[:, _lanes(g)] = w_ref[...].astype(BF16)
    acc = jnp.dot(h_scr[...], w_scr[...], preferred_element_type=F32)

    def head_rms(a, gain):
        ms = jnp.mean(a * a, axis=-1, keepdims=True)
        return a * lax.rsqrt(ms + EPS) * gain

    qz_ref[:, _lanes(0)] = (head_rms(acc[:, _lanes(0)], qg_ref[...]) * HEAD_DIM ** -0.5).astype(BF16)
    kv_ref[:, _lanes(0)] = head_rms(acc[:, _lanes(1)], kg_ref[...]).astype(BF16)
    kv_ref[:, _lanes(1)] = acc[:, _lanes(2)].astype(BF16)
    qz_ref[:, _lanes(1)] = _silu(acc[:, _lanes(3)]).astype(BF16)
    hgp_ref[:, _lanes(0)] = _silu(acc[:, _lanes(4)]).astype(BF16)
    lw = lb_ref[...]
    e = jnp.exp(lw - jnp.max(lw, axis=0, keepdims=True))
    lb = e[0:1, :] / jnp.sum(e, axis=0, keepdims=True)
    fr = acc[:, _lanes(5)]
    logf_ref[...] = jnp.log(lb + (1.0 - lb) * jax.nn.sigmoid(fr))
    hgp_ref[:, _lanes(1)] = ((1.0 - lb) * jax.nn.sigmoid(-fr)).astype(BF16)
    hgp_ref[:, _lanes(2)] = acc[:, _lanes(6)].astype(BF16)
    hgp_ref[:, _lanes(3)] = _silu(acc[:, _lanes(7)]).astype(BF16)


def _proj_call(x2, mod3, norm_g, w_in2, q_g, k_g, lower_bounds, seq_len):
    m, d = x2.shape
    tiles_per_batch = seq_len // PROJ_TM
    grid = (m // PROJ_TM, N_HEADS)

    def w_spec(g):
        return pl.BlockSpec((d, HEAD_DIM), lambda i, j: (0, g * N_HEADS + j))
    return pl.pallas_call(
        _proj_kernel,
        grid=grid,
        in_specs=[
            pl.BlockSpec((PROJ_TM, d), lambda i, j: (i, 0)),
            pl.BlockSpec((None, 1, d), lambda i, j: (i // tiles_per_batch, 0, 0)),
            pl.BlockSpec((None, 1, d), lambda i, j: (i // tiles_per_batch, 0, 1)),
            pl.BlockSpec((1, d), lambda i, j: (0, 0)),
        ] + [w_spec(g) for g in range(N_GROUPS)] + [
            pl.BlockSpec((1, HEAD_DIM), lambda i, j: (0, 0)),
            pl.BlockSpec((1, HEAD_DIM), lambda i, j: (0, 0)),
            pl.BlockSpec((lower_bounds.shape[0], HEAD_DIM), lambda i, j: (0, j)),
        ],
        out_specs=[
            pl.BlockSpec((PROJ_TM, QZ_W), lambda i, j: (i, j)),
            pl.BlockSpec((PROJ_TM, QZ_W), lambda i, j: (i, j)),
            pl.BlockSpec((PROJ_TM, HGP_W), lambda i, j: (i, j)),
            pl.BlockSpec((PROJ_TM, HEAD_DIM), lambda i, j: (i, j)),
        ],
        out_shape=[
            jax.ShapeDtypeStruct((m, N_HEADS * QZ_W), BF16),
            jax.ShapeDtypeStruct((m, N_HEADS * QZ_W), BF16),
            jax.ShapeDtypeStruct((m, N_HEADS * HGP_W), BF16),
            jax.ShapeDtypeStruct((m, GROUP), F32),
        ],
        scratch_shapes=[
            pltpu.VMEM((PROJ_TM, d), BF16),
            pltpu.VMEM((d, N_GROUPS * HEAD_DIM), BF16),
        ],
        compiler_params=pltpu.CompilerParams(
            dimension_semantics=("arbitrary", "arbitrary"), vmem_limit_bytes=VMEM_LIMIT),
        name="proj",
    )(x2, mod3, mod3, norm_g, *([w_in2] * N_GROUPS), q_g, k_g, lower_bounds)


def _build_bias(relb_ref, bias_scr):
    band_w = 4 * CHUNK
    r = lax.broadcasted_iota(jnp.int32, (CHUNK, band_w), 0)
    c = lax.broadcasted_iota(jnp.int32, (CHUNK, band_w), 1)
    dist = 3 * CHUNK + r - c
    n_var = CHUNK - 1 + MAX_REL
    far_w = (LEFT_CHUNKS - 3) * CHUNK
    for h in range(N_HEADS):
        far = relb_ref[h, 2 * MAX_REL]

        def body(d, acc, h=h):
            return jnp.where(dist == d - (CHUNK - 1), relb_ref[h, d + MAX_REL - (CHUNK - 1)], acc)
        near = lax.fori_loop(0, n_var, body, jnp.full((CHUNK, band_w), far, F32))
        for cq in range(ATT_RQ // CHUNK):
            rows = slice(cq * CHUNK, (cq + 1) * CHUNK)
            lo = cq * CHUNK
            bias_scr[h, rows, :] = jnp.full((CHUNK, ATT_WK), -jnp.inf, F32)
            bias_scr[h, rows, lo:lo + far_w] = jnp.full((CHUNK, far_w), far, F32)
            bias_scr[h, rows, lo + far_w:lo + far_w + band_w] = near


def _att_kernel(relb_ref, qz_ref, kv0_ref, kv1_ref, kv2_ref, o_ref, bias_scr):
    b = pl.program_id(0)
    i = pl.program_id(1)

    @pl.when((b == 0) & (i == 0))
    def _():
        _build_bias(relb_ref, bias_scr)

    kpos = lax.broadcasted_iota(jnp.int32, (ATT_RQ, ATT_WK), 1)
    valid = kpos >= (LEFT_CHUNKS * CHUNK - ATT_RQ * i)
    nt = (((1,), (1,)), ((), ()))
    for h in range(N_HEADS):
        qs = slice(h * QZ_W, h * QZ_W + HEAD_DIM)
        zs = slice(h * QZ_W + HEAD_DIM, (h + 1) * QZ_W)
        q = qz_ref[:, qs]
        s = jnp.concatenate(
            [lax.dot_general(q, kr[:, qs], nt, preferred_element_type=F32)
             for kr in (kv0_ref, kv1_ref, kv2_ref)], axis=1)
        s = jnp.where(valid, s + bias_scr[h], -jnp.inf)
        m = jnp.max(s, axis=-1, keepdims=True)
        p = jnp.exp(s - m)
        l = jnp.sum(p, axis=-1, keepdims=True)
        pb = p.astype(BF16)
        o = jnp.dot(pb[:, 0:ATT_RQ], kv0_ref[:, zs], preferred_element_type=F32)
        o += jnp.dot(pb[:, ATT_RQ:2 * ATT_RQ], kv1_ref[:, zs], preferred_element_type=F32)
        o += jnp.dot(pb[:, 2 * ATT_RQ:3 * ATT_RQ], kv2_ref[:, zs], preferred_element_type=F32)
        o = (o / l) * qz_ref[:, zs].astype(F32)
        o_ref[:, _lanes(h)] = o.astype(BF16)


def _att_call(rel_bias, qz, kv, batch, seq_len):
    nq = seq_len // ATT_RQ
    m, w = qz.shape

    def kv_spec(back):
        return pl.BlockSpec((ATT_RQ, w), lambda b, i: (b * nq + jnp.maximum(i - back, 0), 0))
    return pl.pallas_call(
        _att_kernel,
        grid=(batch, nq),
        in_specs=[
            pl.BlockSpec(memory_space=pltpu.SMEM),
            pl.BlockSpec((ATT_RQ, w), lambda b, i: (b * nq + i, 0)),
            kv_spec(2), kv_spec(1), kv_spec(0),
        ],
        out_specs=pl.BlockSpec((ATT_RQ, GROUP), lambda b, i: (b * nq + i, 0)),
        out_shape=jax.ShapeDtypeStruct((m, GROUP), BF16),
        scratch_shapes=[pltpu.VMEM((N_HEADS, ATT_RQ, ATT_WK), F32)],
        compiler_params=pltpu.CompilerParams(
            dimension_semantics=("arbitrary", "arbitrary"), vmem_limit_bytes=VMEM_LIMIT),
        name="att",
    )(rel_bias, qz, kv, kv, kv)


def _hg_kernel(p_ref, logf_ref, og_ref, o_ref,
               st_scr, intra_scr, kf_scr, vf_scr, bf_scr):
    t = pl.program_id(1)

    @pl.when(t == 0)
    def _():
        st_scr[...] = jnp.zeros_like(st_scr)

    row_i = lax.broadcasted_iota(jnp.int32, (CHUNK, CHUNK), 0)
    col_i = lax.broadcasted_iota(jnp.int32, (CHUNK, CHUNK), 1)
    tril = row_i >= col_i
    tri_bf = jnp.where(tril, 1.0, 0.0).astype(BF16)
    mid = CHUNK // 2 - 1
    nt = (((1,), (1,)), ((), ()))
    tn = (((0,), (0,)), ((), ()))
    og = og_ref[...]

    def piece(h, k):
        return slice(h * HGP_W + k * HEAD_DIM, h * HGP_W + (k + 1) * HEAD_DIM)

    def chunk_body(ci, carry):
        rows = pl.ds(pl.multiple_of(ci * CHUNK, CHUNK), CHUNK)
        lf = logf_ref[rows, :]
        hi = lf.astype(BF16)
        r1 = lf - hi.astype(F32)
        md = r1.astype(BF16)
        lo = (r1 - md.astype(F32)).astype(BF16)
        bcum = (jnp.dot(tri_bf, lo, preferred_element_type=F32)
                + jnp.dot(tri_bf, md, preferred_element_type=F32)
                + jnp.dot(tri_bf, hi, preferred_element_type=F32))
        b_first = bcum[0:1, :]
        b_mid = bcum[mid:mid + 1, :]
        b_last = bcum[CHUNK - 1:CHUNK, :]
        steep = jnp.max(jnp.maximum(b_first - b_mid, b_mid - b_last)) > EXP_CLAMP

        inters = []
        for h in range(N_HEADS):
            hs = _lanes(h)
            qt = p_ref[rows, piece(h, 0)].astype(F32)
            kh = p_ref[rows, piece(h, 1)].astype(F32)
            vh = p_ref[rows, piece(h, 2)]
            bh = bcum[:, hs]
            bm = b_mid[:, hs]
            bl = b_last[:, hs]
            st = st_scr[h]
            qb = (qt * jnp.exp(bh)).astype(BF16)
            inters.append(lax.dot_general(qb, st.astype(BF16), nt, preferred_element_type=F32))
            qp = (qt * jnp.exp(jnp.minimum(bh - bm, EXP_CLAMP))).astype(BF16)
            kp = (kh * jnp.exp(jnp.minimum(bm - bh, EXP_CLAMP))).astype(BF16)
            a = lax.dot_general(qp, kp, nt, preferred_element_type=F32)
            a = jnp.where(tril, a, 0.0).astype(BF16)
            intra_scr[:, hs] = jnp.dot(a, vh, preferred_element_type=F32)
            kdec = (kh * jnp.exp(bl - bh)).astype(BF16)
            upd = lax.dot_general(vh, kdec, tn, preferred_element_type=F32)
            st_scr[h] = st * jnp.exp(bl) + upd

        @pl.when(steep)
        def _():
            for h in range(N_HEADS):
                kf_scr[:, _lanes(h)] = p_ref[rows, piece(h, 1)].astype(F32)
                vf_scr[:, _lanes(h)] = p_ref[rows, piece(h, 2)].astype(F32)
            bf_scr[...] = bcum
            intra_scr[...] = jnp.zeros_like(intra_scr)
            t_idx = lax.broadcasted_iota(jnp.int32, (CHUNK, HEAD_DIM), 0)

            def pair_body(s, c2):
                k_all = kf_scr[pl.ds(s, 1), :]
                v_all = vf_scr[pl.ds(s, 1), :]
                b_all = bf_scr[pl.ds(s, 1), :]
                for h in range(N_HEADS):
                    hs = _lanes(h)
                    w = jnp.exp(jnp.minimum(bf_scr[:, hs] - b_all[:, hs], 0.0))
                    qt = p_ref[rows, piece(h, 0)].astype(F32)
                    p = jnp.where(t_idx >= s, qt * k_all[:, hs] * w, 0.0)
                    col = jnp.sum(p, axis=-1, keepdims=True)
                    intra_scr[:, hs] += col * v_all[:, hs]
                return c2
            lax.fori_loop(0, CHUNK, pair_body, 0)

        for h in range(N_HEADS):
            hs = _lanes(h)
            o = inters[h] + intra_scr[:, hs]
            ms = jnp.mean(o * o, axis=-1, keepdims=True)
            y = o * lax.rsqrt(ms + EPS) * og
            o_ref[rows, hs] = (y * p_ref[rows, piece(h, 3)].astype(F32)).astype(BF16)
        return carry

    lax.fori_loop(0, HG_RB // CHUNK, chunk_body, 0)


def _hg_call(hgp, logf, o_g, batch, seq_len):
    nt_ = seq_len // HG_RB
    m, w = hgp.shape
    return pl.pallas_call(
        _hg_kernel,
        grid=(batch, nt_),
        in_specs=[
            pl.BlockSpec((HG_RB, w), lambda b, t: (b * nt_ + t, 0)),
            pl.BlockSpec((HG_RB, GROUP), lambda b, t: (b * nt_ + t, 0)),
            pl.BlockSpec((1, HEAD_DIM), lambda b, t: (0, 0)),
        ],
        out_specs=pl.BlockSpec((HG_RB, GROUP), lambda b, t: (b * nt_ + t, 0)),
        out_shape=jax.ShapeDtypeStruct((m, GROUP), BF16),
        scratch_shapes=[
            pltpu.VMEM((N_HEADS, HEAD_DIM, HEAD_DIM), F32),
            pltpu.VMEM((CHUNK, GROUP), F32),
            pltpu.VMEM((CHUNK, GROUP), F32),
            pltpu.VMEM((CHUNK, GROUP), F32),
            pltpu.VMEM((CHUNK, GROUP), F32),
        ],
        compiler_params=pltpu.CompilerParams(
            dimension_semantics=("arbitrary", "arbitrary"), vmem_limit_bytes=VMEM_LIMIT),
        name="hgrn",
    )(hgp, logf, o_g)


def _out_kernel(att_ref, hg_ref, w_ref, x_ref, gate_ref, o_ref, w_scr):
    @pl.when(pl.program_id(0) == 0)
    def _():
        w_scr[...] = w_ref[...].astype(BF16)

    y = jnp.dot(att_ref[...], w_scr[0:GROUP, :], preferred_element_type=F32)
    y += jnp.dot(hg_ref[...], w_scr[GROUP:2 * GROUP, :], preferred_element_type=F32)
    o_ref[...] = x_ref[...] + gate_ref[...] * y


def _out_call(att, hg, w_out2, x2, mod3, seq_len):
    m, d = x2.shape
    tiles_per_batch = seq_len // OUT_TM
    return pl.pallas_call(
        _out_kernel,
        grid=(m // OUT_TM,),
        in_specs=[
            pl.BlockSpec((OUT_TM, GROUP), lambda i: (i, 0)),
            pl.BlockSpec((OUT_TM, GROUP), lambda i: (i, 0)),
            pl.BlockSpec(w_out2.shape, lambda i: (0, 0), pipeline_mode=pl.Buffered(1)),
            pl.BlockSpec((OUT_TM, d), lambda i: (i, 0)),
            pl.BlockSpec((None, 1, d), lambda i: (i // tiles_per_batch, 0, 2)),
        ],
        out_specs=pl.BlockSpec((OUT_TM, d), lambda i: (i, 0)),
        out_shape=jax.ShapeDtypeStruct((m, d), F32),
        scratch_shapes=[pltpu.VMEM(w_out2.shape, BF16)],
        compiler_params=pltpu.CompilerParams(
            dimension_semantics=("arbitrary",), vmem_limit_bytes=VMEM_LIMIT),
        name="outproj",
    )(att, hg, w_out2, x2, mod3)


def kernel(x, c, norm_g, w_ada, b_ada, w_in, q_norm_g, k_norm_g, rel_bias,
           lower_bounds, hg_norm_g, w_out):
    batch, seq_len, d = x.shape
    assert w_in.shape == (1, d, N_GROUPS * GROUP) and w_out.shape == (1, 2 * GROUP, d)
    assert seq_len % PROJ_TM == 0 and seq_len % ATT_RQ == 0 and seq_len % HG_RB == 0
    assert rel_bias.shape == (1, N_HEADS, 2 * MAX_REL + 1)

    x2 = x.reshape(batch * seq_len, d)
    c_pad = jnp.pad(c, ((0, 8 - batch), (0, 0)))
    mod = _ada_call(c_pad, w_ada, b_ada)
    mod3 = mod[:batch].reshape(batch, 1, 3 * d)

    qz, kv, hgp, logf = _proj_call(x2, mod3, norm_g, w_in[0], q_norm_g, k_norm_g,
                                   lower_bounds, seq_len)
    att = _att_call(rel_bias[0], qz, kv, batch, seq_len)
    hg = _hg_call(hgp, logf, hg_norm_g, batch, seq_len)
    out = _out_call(att, hg, w_out[0], x2, mod3, seq_len)
    return out.reshape(batch, seq_len, d)
```

```python
import jax
import jax.numpy as jnp
from jax import lax
from jax.experimental import pallas as pl
from jax.experimental.pallas import tpu as pltpu

F32 = jnp.float32
BF16 = jnp.bfloat16

CHUNK = 64
HEAD_DIM = 128
N_HEADS = 8
GROUP = N_HEADS * HEAD_DIM
N_GROUPS = 8
LEFT_CHUNKS = 8
MAX_REL = 128
EPS = 1e-6
EXP_CLAMP = 80.0

VMEM_LIMIT = 56 * 1024 * 1024

ADA_TN = 768
PROJ_TM = 1024
NORM_RB = 128
ATT_RQ = 256
ATT_WK = ATT_RQ + LEFT_CHUNKS * CHUNK
HG_RB = 256
OUT_TM = 512

QZ_W = 2 * HEAD_DIM
HGP_W = 4 * HEAD_DIM


def _silu(v):
    return v * jax.nn.sigmoid(v)


def _lanes(piece, width=HEAD_DIM):
    return slice(piece * width, (piece + 1) * width)


def _ada_kernel(c_ref, w_ref, b_ref, o_ref):
    ca = _silu(c_ref[...])
    o_ref[...] = jnp.dot(ca, w_ref[...], preferred_element_type=F32,
                         precision=lax.Precision.HIGHEST) + b_ref[...]


def _ada_call(c_pad, w_ada, b_ada):
    rows, d = c_pad.shape
    n = w_ada.shape[-1]
    return pl.pallas_call(
        _ada_kernel,
        grid=(n // ADA_TN,),
        in_specs=[
            pl.BlockSpec((rows, d), lambda j: (0, 0)),
            pl.BlockSpec((None, d, ADA_TN), lambda j: (0, 0, j)),
            pl.BlockSpec((1, ADA_TN), lambda j: (0, j)),
        ],
        out_specs=pl.BlockSpec((rows, ADA_TN), lambda j: (0, j)),
        out_shape=jax.ShapeDtypeStruct((rows, n), F32),
        compiler_params=pltpu.CompilerParams(
            dimension_semantics=("arbitrary",), vmem_limit_bytes=VMEM_LIMIT),
        name="ada",
    )(c_pad, w_ada, b_ada)


def _proj_kernel(x_ref, shift_ref, scale_ref, ng_ref,
                 w0, w1, w2, w3, w4, w5, w6, w7,
                 qg_ref, kg_ref, lb_ref,
                 qz_ref, kv_ref, hgp_ref, logf_ref, h_scr, w_scr):
    j = pl.program_id(1)
    tm = x_ref.shape[0]

    @pl.when(j == 0)
    def _():
        g = ng_ref[...]
        sc = 1.0 + scale_ref[...]
        sh = shift_ref[...]

        def body(r, carry):
            rows = pl.ds(pl.multiple_of(r * NORM_RB, NORM_RB), NORM_RB)
            xv = x_ref[rows, :]
            ms = jnp.mean(xv * xv, axis=-1, keepdims=True)
            y = xv * lax.rsqrt(ms + EPS) * g
            h_scr[rows, :] = (y * sc + sh).astype(BF16)
            return carry
        lax.fori_loop(0, tm // NORM_RB, body, 0)

    for g, w_ref in enumerate((w0, w1, w2, w3, w4, w5, w6, w7)):
        w_scr[:, _lanes(g)] = w_ref[...].astype(BF16)
    acc = jnp.dot(h_scr[...], w_scr[...], preferred_element_type=F32)

    def head_rms(a, gain):
        ms = jnp.mean(a * a, axis=-1, keepdims=True)
        return a * lax.rsqrt(ms + EPS) * gain

    qz_ref[:, _lanes(0)] = (head_rms(acc[:, _lanes(0)], qg_ref[...]) * HEAD_DIM ** -0.5).astype(BF16)
    kv_ref[:, _lanes(0)] = head_rms(acc[:, _lanes(1)], kg_ref[...]).astype(BF16)
    kv_ref[:, _lanes(1)] = acc[:, _lanes(2)].astype(BF16)
    qz_ref[:, _lanes(1)] = _silu(acc[:, _lanes(3)]).astype(BF16)
    hgp_ref[:, _lanes(0)] = _silu(acc[:, _lanes(4)]).astype(BF16)
    lw = lb_ref[...]
    e = jnp.exp(lw - jnp.max(lw, axis=0, keepdims=True))
    lb = e[0:1, :] / jnp.sum(e, axis=0, keepdims=True)
    fr = acc[:, _lanes(5)]
    logf_ref[...] = jnp.log(lb + (1.0 - lb) * jax.nn.sigmoid(fr))
    hgp_ref[:, _lanes(1)] = ((1.0 - lb) * jax.nn.sigmoid(-fr)).astype(BF16)
    hgp_ref[:, _lanes(2)] = acc[:, _lanes(6)].astype(BF16)
    hgp_ref[:, _lanes(3)] = _silu(acc[:, _lanes(7)]).astype(BF16)


def _proj_call(x2, mod3, norm_g, w_in2, q_g, k_g, lower_bounds, seq_len):
    m, d = x2.shape
    tiles_per_batch = seq_len // PROJ_TM
    grid = (m // PROJ_TM, N_HEADS)

    def w_spec(g):
        return pl.BlockSpec((d, HEAD_DIM), lambda i, j: (0, g * N_HEADS + j))
    return pl.pallas_call(
        _proj_kernel,
        grid=grid,
        in_specs=[
            pl.BlockSpec((PROJ_TM, d), lambda i, j: (i, 0)),
            pl.BlockSpec((None, 1, d), lambda i, j: (i // tiles_per_batch, 0, 0)),
            pl.BlockSpec((None, 1, d), lambda i, j: (i // tiles_per_batch, 0, 1)),
            pl.BlockSpec((1, d), lambda i, j: (0, 0)),
        ] + [w_spec(g) for g in range(N_GROUPS)] + [
            pl.BlockSpec((1, HEAD_DIM), lambda i, j: (0, 0)),
            pl.BlockSpec((1, HEAD_DIM), lambda i, j: (0, 0)),
            pl.BlockSpec((lower_bounds.shape[0], HEAD_DIM), lambda i, j: (0, j)),
        ],
        out_specs=[
            pl.BlockSpec((PROJ_TM, QZ_W), lambda i, j: (i, j)),
            pl.BlockSpec((PROJ_TM, QZ_W), lambda i, j: (i, j)),
            pl.BlockSpec((PROJ_TM, HGP_W), lambda i, j: (i, j)),
            pl.BlockSpec((PROJ_TM, HEAD_DIM), lambda i, j: (i, j)),
        ],
        out_shape=[
            jax.ShapeDtypeStruct((m, N_HEADS * QZ_W), BF16),
            jax.ShapeDtypeStruct((m, N_HEADS * QZ_W), BF16),
            jax.ShapeDtypeStruct((m, N_HEADS * HGP_W), BF16),
            jax.ShapeDtypeStruct((m, GROUP), F32),
        ],
        scratch_shapes=[
            pltpu.VMEM((PROJ_TM, d), BF16),
            pltpu.VMEM((d, N_GROUPS * HEAD_DIM), BF16),
        ],
        compiler_params=pltpu.CompilerParams(
            dimension_semantics=("arbitrary", "arbitrary"), vmem_limit_bytes=VMEM_LIMIT),
        name="proj",
    )(x2, mod3, mod3, norm_g, *([w_in2] * N_GROUPS), q_g, k_g, lower_bounds)


def _build_bias(relb_ref, bias_scr):
    band_w = 4 * CHUNK
    r = lax.broadcasted_iota(jnp.int32, (CHUNK, band_w), 0)
    c = lax.broadcasted_iota(jnp.int32, (CHUNK, band_w), 1)
    dist = 3 * CHUNK + r - c
    n_var = CHUNK - 1 + MAX_REL
    far_w = (LEFT_CHUNKS - 3) * CHUNK
    for h in range(N_HEADS):
        far = relb_ref[h, 2 * MAX_REL]

        def body(d, acc, h=h):
            return jnp.where(dist == d - (CHUNK - 1), relb_ref[h, d + MAX_REL - (CHUNK - 1)], acc)
        near = lax.fori_loop(0, n_var, body, jnp.full((CHUNK, band_w), far, F32))
        for cq in range(ATT_RQ // CHUNK):
            rows = slice(cq * CHUNK, (cq + 1) * CHUNK)
            lo = cq * CHUNK
            bias_scr[h, rows, :] = jnp.full((CHUNK, ATT_WK), -jnp.inf, F32)
            bias_scr[h, rows, lo:lo + far_w] = jnp.full((CHUNK, far_w), far, F32)
            bias_scr[h, rows, lo + far_w:lo + far_w + band_w] = near


def _att_kernel(relb_ref, qz_ref, kv0_ref, kv1_ref, kv2_ref, o_ref, bias_scr):
    b = pl.program_id(0)
    i = pl.program_id(1)

    @pl.when((b == 0) & (i == 0))
    def _():
        _build_bias(relb_ref, bias_scr)

    kpos = lax.broadcasted_iota(jnp.int32, (ATT_RQ, ATT_WK), 1)
    valid = kpos >= (LEFT_CHUNKS * CHUNK - ATT_RQ * i)
    nt = (((1,), (1,)), ((), ()))
    for h in range(N_HEADS):
        qs = slice(h * QZ_W, h * QZ_W + HEAD_DIM)
        zs = slice(h * QZ_W + HEAD_DIM, (h + 1) * QZ_W)
        q = qz_ref[:, qs]
        s = jnp.concatenate(
            [lax.dot_general(q, kr[:, qs], nt, preferred_element_type=F32)
             for kr in (kv0_ref, kv1_ref, kv2_ref)], axis=1)
        s = jnp.where(valid, s + bias_scr[h], -jnp.inf)
        m = jnp.max(s, axis=-1, keepdims=True)
        p = jnp.exp(s - m)
        l = jnp.sum(p, axis=-1, keepdims=True)
        pb = p.astype(BF16)
        o = jnp.dot(pb[:, 0:ATT_RQ], kv0_ref[:, zs], preferred_element_type=F32)
        o += jnp.dot(pb[:, ATT_RQ:2 * ATT_RQ], kv1_ref[:, zs], preferred_element_type=F32)
        o += jnp.dot(pb[:, 2 * ATT_RQ:3 * ATT_RQ], kv2_ref[:, zs], preferred_element_type=F32)
        o = (o / l) * qz_ref[:, zs].astype(F32)
        o_ref[:, _lanes(h)] = o.astype(BF16)


def _att_call(rel_bias, qz, kv, batch, seq_len):
    nq = seq_len // ATT_RQ
    m, w = qz.shape

    def kv_spec(back):
        return pl.BlockSpec((ATT_RQ, w), lambda b, i: (b * nq + jnp.maximum(i - back, 0), 0))
    return pl.pallas_call(
        _att_kernel,
        grid=(batch, nq),
        in_specs=[
            pl.BlockSpec(memory_space=pltpu.SMEM),
            pl.BlockSpec((ATT_RQ, w), lambda b, i: (b * nq + i, 0)),
            kv_spec(2), kv_spec(1), kv_spec(0),
        ],
        out_specs=pl.BlockSpec((ATT_RQ, GROUP), lambda b, i: (b * nq + i, 0)),
        out_shape=jax.ShapeDtypeStruct((m, GROUP), BF16),
        scratch_shapes=[pltpu.VMEM((N_HEADS, ATT_RQ, ATT_WK), F32)],
        compiler_params=pltpu.CompilerParams(
            dimension_semantics=("arbitrary", "arbitrary"), vmem_limit_bytes=VMEM_LIMIT),
        name="att",
    )(rel_bias, qz, kv, kv, kv)


def _hg_piece(h, k):
    return slice(h * HGP_W + k * HEAD_DIM, h * HGP_W + (k + 1) * HEAD_DIM)


def _per_chunk_rows(rows, width):
    return jnp.concatenate([jnp.broadcast_to(r, (CHUNK, width)) for r in rows], axis=0)


def _hg_kernel(p_ref, logf_ref, og_ref, o_ref,
               st_scr, s0_scr, qf_scr, kf_scr, vf_scr, bf_scr, acc_scr):
    t = pl.program_id(1)

    @pl.when(t == 0)
    def _():
        st_scr[...] = jnp.zeros_like(st_scr)

    rb = HG_RB
    c = CHUNK
    row_i = lax.broadcasted_iota(jnp.int32, (rb, rb), 0)
    col_i = lax.broadcasted_iota(jnp.int32, (rb, rb), 1)
    tril = row_i >= col_i
    tri_bf = jnp.where(tril, 1.0, 0.0).astype(BF16)
    same_chunk_tril = tril & ((row_i // c) == (col_i // c))
    nt = (((1,), (1,)), ((), ()))
    tn = (((0,), (0,)), ((), ()))
    og = og_ref[...]

    lf = logf_ref[...]
    hi = lf.astype(BF16)
    r1 = lf - hi.astype(F32)
    md = r1.astype(BF16)
    lo = (r1 - md.astype(F32)).astype(BF16)
    beta = (jnp.dot(tri_bf, lo, preferred_element_type=F32)
            + jnp.dot(tri_bf, md, preferred_element_type=F32)
            + jnp.dot(tri_bf, hi, preferred_element_type=F32))

    mids = [beta[k * c + c // 2 - 1:k * c + c // 2, :] for k in range(4)]
    ends = [beta[k * c + c - 1:k * c + c, :] for k in range(4)]
    x0 = beta - _per_chunk_rows(mids, GROUP)
    e1 = jnp.exp(-jnp.abs(beta - _per_chunk_rows([ends[0], ends[0], ends[2], ends[2]], GROUP)))
    e2 = jnp.exp(-jnp.abs(beta - ends[1]))
    eq = jnp.exp(beta)
    ek = jnp.exp(ends[3] - beta)
    g_blk = jnp.exp(ends[3])
    edge = jnp.concatenate([x0[k * c:k * c + 1, :] for k in range(4)]
                           + [x0[k * c + c - 1:k * c + c, :] for k in range(4)], axis=0)
    steep = jnp.max(jnp.abs(edge)) > EXP_CLAMP

    z64 = jnp.zeros((c, HEAD_DIM), BF16)

    def cross_scores(qt, kh, hs):
        def scaled(v, e, lo_row, hi_row):
            return (v[lo_row:hi_row] * e[lo_row:hi_row, hs]).astype(BF16)
        lhs = jnp.concatenate([
            jnp.concatenate([z64, scaled(qt, e1, c, 2 * c), z64, z64], axis=0),
            jnp.concatenate([z64, z64, z64, scaled(qt, e1, 3 * c, 4 * c)], axis=0),
            jnp.concatenate([z64, z64, scaled(qt, e2, 2 * c, 4 * c)], axis=0)], axis=1)
        rhs = jnp.concatenate([
            jnp.concatenate([scaled(kh, e1, 0, c), z64, z64, z64], axis=0),
            jnp.concatenate([z64, z64, scaled(kh, e1, 2 * c, 3 * c), z64], axis=0),
            jnp.concatenate([scaled(kh, e2, 0, 2 * c), z64, z64], axis=0)], axis=1)
        return lax.dot_general(lhs, rhs, nt, preferred_element_type=F32)

    def carried(qt, st, hs):
        qb = (qt * eq[:, hs]).astype(BF16)
        return lax.dot_general(qb, st.astype(BF16), nt, preferred_element_type=F32)

    def write_out(h, o):
        ms = jnp.mean(o * o, axis=-1, keepdims=True)
        y = o * lax.rsqrt(ms + EPS) * og
        o_ref[:, _lanes(h)] = (y * p_ref[:, _hg_piece(h, 3)].astype(F32)).astype(BF16)

    for h in range(N_HEADS):
        hs = _lanes(h)
        qt = p_ref[:, _hg_piece(h, 0)].astype(F32)
        kh = p_ref[:, _hg_piece(h, 1)].astype(F32)
        vh = p_ref[:, _hg_piece(h, 2)]
        st = st_scr[h]
        s0_scr[h] = st
        xh = x0[:, hs]
        q0 = (qt * jnp.exp(jnp.minimum(xh, EXP_CLAMP))).astype(BF16)
        k0 = (kh * jnp.exp(jnp.minimum(-xh, EXP_CLAMP))).astype(BF16)
        a0 = lax.dot_general(q0, k0, nt, preferred_element_type=F32)
        a = jnp.where(same_chunk_tril, a0, cross_scores(qt, kh, hs)).astype(BF16)
        o = jnp.dot(a, vh, preferred_element_type=F32) + carried(qt, st, hs)
        kd = (kh * ek[:, hs]).astype(BF16)
        st_scr[h] = st * g_blk[:, hs] + lax.dot_general(vh, kd, tn, preferred_element_type=F32)
        write_out(h, o)

    @pl.when(steep)
    def _():
        for h in range(N_HEADS):
            qf_scr[:, _lanes(h)] = p_ref[:, _hg_piece(h, 0)].astype(F32)
            kf_scr[:, _lanes(h)] = p_ref[:, _hg_piece(h, 1)].astype(F32)
            vf_scr[:, _lanes(h)] = p_ref[:, _hg_piece(h, 2)].astype(F32)
        bf_scr[...] = beta
        acc_scr[...] = jnp.zeros_like(acc_scr)
        t_idx = lax.broadcasted_iota(jnp.int32, (c, HEAD_DIM), 0)

        def pair_body(s, carry):
            base = pl.multiple_of((s // c) * c, c)
            rows = pl.ds(base, c)
            k_all = kf_scr[pl.ds(s, 1), :]
            v_all = vf_scr[pl.ds(s, 1), :]
            b_all = bf_scr[pl.ds(s, 1), :]
            for h in range(N_HEADS):
                hs = _lanes(h)
                w = jnp.exp(jnp.minimum(bf_scr[rows, hs] - b_all[:, hs], 0.0))
                p = jnp.where(t_idx >= s - base, qf_scr[rows, hs] * k_all[:, hs] * w, 0.0)
                acc_scr[rows, hs] += jnp.sum(p, axis=-1, keepdims=True) * v_all[:, hs]
            return carry
        lax.fori_loop(0, rb, pair_body, 0)

        for h in range(N_HEADS):
            hs = _lanes(h)
            qt = p_ref[:, _hg_piece(h, 0)].astype(F32)
            kh = p_ref[:, _hg_piece(h, 1)].astype(F32)
            vh = p_ref[:, _hg_piece(h, 2)]
            a = cross_scores(qt, kh, hs).astype(BF16)
            o = (jnp.dot(a, vh, preferred_element_type=F32) + carried(qt, s0_scr[h], hs)
                 + acc_scr[:, hs])
            write_out(h, o)


def _hg_call(hgp, logf, o_g, batch, seq_len):
    nt_ = seq_len // HG_RB
    m, w = hgp.shape
    return pl.pallas_call(
        _hg_kernel,
        grid=(batch, nt_),
        in_specs=[
            pl.BlockSpec((HG_RB, w), lambda b, t: (b * nt_ + t, 0)),
            pl.BlockSpec((HG_RB, GROUP), lambda b, t: (b * nt_ + t, 0)),
            pl.BlockSpec((1, HEAD_DIM), lambda b, t: (0, 0)),
        ],
        out_specs=pl.BlockSpec((HG_RB, GROUP), lambda b, t: (b * nt_ + t, 0)),
        out_shape=jax.ShapeDtypeStruct((m, GROUP), BF16),
        scratch_shapes=[
            pltpu.VMEM((N_HEADS, HEAD_DIM, HEAD_DIM), F32),
            pltpu.VMEM((N_HEADS, HEAD_DIM, HEAD_DIM), F32),
        ] + [pltpu.VMEM((HG_RB, GROUP), F32)] * 5,
        compiler_params=pltpu.CompilerParams(
            dimension_semantics=("arbitrary", "arbitrary"), vmem_limit_bytes=VMEM_LIMIT),
        name="hgrn",
    )(hgp, logf, o_g)


def _out_kernel(att_ref, hg_ref, w_ref, x_ref, gate_ref, o_ref, w_scr):
    @pl.when(pl.program_id(0) == 0)
    def _():
        w_scr[...] = w_ref[...].astype(BF16)

    y = jnp.dot(att_ref[...], w_scr[0:GROUP, :], preferred_element_type=F32)
    y += jnp.dot(hg_ref[...], w_scr[GROUP:2 * GROUP, :], preferred_element_type=F32)
    o_ref[...] = x_ref[...] + gate_ref[...] * y


def _out_call(att, hg, w_out2, x2, mod3, seq_len):
    m, d = x2.shape
    tiles_per_batch = seq_len // OUT_TM
    return pl.pallas_call(
        _out_kernel,
        grid=(m // OUT_TM,),
        in_specs=[
            pl.BlockSpec((OUT_TM, GROUP), lambda i: (i, 0)),
            pl.BlockSpec((OUT_TM, GROUP), lambda i: (i, 0)),
            pl.BlockSpec(w_out2.shape, lambda i: (0, 0), pipeline_mode=pl.Buffered(1)),
            pl.BlockSpec((OUT_TM, d), lambda i: (i, 0)),
            pl.BlockSpec((None, 1, d), lambda i: (i // tiles_per_batch, 0, 2)),
        ],
        out_specs=pl.BlockSpec((OUT_TM, d), lambda i: (i, 0)),
        out_shape=jax.ShapeDtypeStruct((m, d), F32),
        scratch_shapes=[pltpu.VMEM(w_out2.shape, BF16)],
        compiler_params=pltpu.CompilerParams(
            dimension_semantics=("arbitrary",), vmem_limit_bytes=VMEM_LIMIT),
        name="outproj",
    )(att, hg, w_out2, x2, mod3)


def kernel(x, c, norm_g, w_ada, b_ada, w_in, q_norm_g, k_norm_g, rel_bias,
           lower_bounds, hg_norm_g, w_out):
    batch, seq_len, d = x.shape
    assert w_in.shape == (1, d, N_GROUPS * GROUP) and w_out.shape == (1, 2 * GROUP, d)
    assert seq_len % PROJ_TM == 0 and seq_len % ATT_RQ == 0 and seq_len % HG_RB == 0
    assert rel_bias.shape == (1, N_HEADS, 2 * MAX_REL + 1)

    x2 = x.reshape(batch * seq_len, d)
    c_pad = jnp.pad(c, ((0, 8 - batch), (0, 0)))
    mod = _ada_call(c_pad, w_ada, b_ada)
    mod3 = mod[:batch].reshape(batch, 1, 3 * d)

    qz, kv, hgp, logf = _proj_call(x2, mod3, norm_g, w_in[0], q_norm_g, k_norm_g,
                                   lower_bounds, seq_len)
    att = _att_call(rel_bias[0], qz, kv, batch, seq_len)
    hg = _hg_call(hgp, logf, hg_norm_g, batch, seq_len)
    out = _out_call(att, hg, w_out[0], x2, mod3, seq_len)
    return out.reshape(batch, seq_len, d)
```

```python
import jax
import jax.numpy as jnp
from jax import lax
from jax.experimental import pallas as pl
from jax.experimental.pallas import tpu as pltpu

F32 = jnp.float32
BF16 = jnp.bfloat16

CHUNK = 64
HEAD_DIM = 128
N_HEADS = 8
GROUP = N_HEADS * HEAD_DIM
N_GROUPS = 8
LEFT_CHUNKS = 8
MAX_REL = 128
EPS = 1e-6
EXP_CLAMP = 80.0

VMEM_LIMIT = 56 * 1024 * 1024

ADA_TN = 768
PROJ_TM = 512
PROJ_HEADS_PER_PASS = 4
PROJ_VMEM_LIMIT = 60 * 1024 * 1024
PROJ_COL_ORDER = (5, 4, 0, 1, 7, 3, 2, 6)
ATT_RQ = 256
ATT_WK = ATT_RQ + LEFT_CHUNKS * CHUNK
HG_RB = 256
OUT_TM = 512

QZ_W = 2 * HEAD_DIM
HGP_W = 4 * HEAD_DIM


def _sigmoid(v):
    return 0.5 * jnp.tanh(0.5 * v) + 0.5


def _silu(v):
    return v * _sigmoid(v)


def _lanes(piece, width=HEAD_DIM):
    return slice(piece * width, (piece + 1) * width)


def _ada_kernel(c_ref, w_ref, b_ref, o_ref):
    ca = _silu(c_ref[...])
    o_ref[...] = jnp.dot(ca, w_ref[...], preferred_element_type=F32,
                         precision=lax.Precision.HIGHEST) + b_ref[...]


def _ada_call(c_pad, w_ada, b_ada):
    rows, d = c_pad.shape
    n = w_ada.shape[-1]
    return pl.pallas_call(
        _ada_kernel,
        grid=(n // ADA_TN,),
        in_specs=[
            pl.BlockSpec((rows, d), lambda j: (0, 0)),
            pl.BlockSpec((None, d, ADA_TN), lambda j: (0, 0, j)),
            pl.BlockSpec((1, ADA_TN), lambda j: (0, j)),
        ],
        out_specs=pl.BlockSpec((rows, ADA_TN), lambda j: (0, j)),
        out_shape=jax.ShapeDtypeStruct((rows, n), F32),
        compiler_params=pltpu.CompilerParams(
            dimension_semantics=("arbitrary",), vmem_limit_bytes=VMEM_LIMIT),
        name="ada",
    )(c_pad, w_ada, b_ada)


def _proj_kernel(xn_ref, x0_ref, shn_ref, scn_ref, sh0_ref, sc0_ref, ng_ref,
                 w0, w1, w2, w3, w4, w5, w6, w7,
                 qg_ref, kg_ref, lb_ref,
                 qz_ref, kv_ref, hgp_ref, logf_ref, ha_scr, hb_scr, w_scr):
    c = pl.program_id(0)
    i = pl.program_id(1)
    j = pl.program_id(2)
    tm = xn_ref.shape[0]
    slab = tm // PROJ_HEADS_PER_PASS
    gain = ng_ref[...]

    def normed(xv, sc_ref, sh_ref):
        ms = jnp.mean(xv * xv, axis=-1, keepdims=True)
        y = xv * lax.rsqrt(ms + EPS) * gain
        return (y * (1.0 + sc_ref[...]) + sh_ref[...]).astype(BF16)

    @pl.when((c == 0) & (i == 0) & (j == 0))
    def _():
        def body(r, carry):
            rows = pl.ds(pl.multiple_of(r * slab, slab), slab)
            ha_scr[rows, :] = normed(x0_ref[rows, :], sc0_ref, sh0_ref)
            return carry
        lax.fori_loop(0, PROJ_HEADS_PER_PASS, body, 0)

    @pl.when(i == 0)
    def _():
        w_refs = (w0, w1, w2, w3, w4, w5, w6, w7)
        for pos, g in enumerate(PROJ_COL_ORDER):
            w_scr[j, :, _lanes(pos)] = w_refs[g][...].astype(BF16)

    def head_rms(a, gain):
        ms = jnp.mean(a * a, axis=-1, keepdims=True)
        return a * lax.rsqrt(ms + EPS) * gain

    def step(h_cur, h_nxt):
        rows = pl.ds(pl.multiple_of(j * slab, slab), slab)
        h_nxt[rows, :] = normed(xn_ref[rows, :], scn_ref, shn_ref)
        acc = jnp.dot(h_cur[...], w_scr[j], preferred_element_type=F32)

        def group(g):
            return acc[:, _lanes(PROJ_COL_ORDER.index(g))]
        lw = lb_ref[...]
        e = jnp.exp(lw - jnp.max(lw, axis=0, keepdims=True))
        lb = e[0:1, :] / jnp.sum(e, axis=0, keepdims=True)
        sg = _sigmoid(group(5))
        logf_ref[...] = jnp.log(lb + (1.0 - lb) * sg)
        hgp_ref[:, _lanes(1)] = ((1.0 - lb) * (1.0 - sg)).astype(BF16)
        hgp_ref[:, _lanes(0)] = _silu(group(4)).astype(BF16)
        hgp_ref[:, _lanes(3)] = _silu(group(7)).astype(BF16)
        hgp_ref[:, _lanes(2)] = group(6).astype(BF16)
        qz_ref[:, _lanes(1)] = _silu(group(3)).astype(BF16)
        qz_ref[:, _lanes(0)] = (head_rms(group(0), qg_ref[...]) * HEAD_DIM ** -0.5).astype(BF16)
        kv_ref[:, _lanes(0)] = head_rms(group(1), kg_ref[...]).astype(BF16)
        kv_ref[:, _lanes(1)] = group(2).astype(BF16)

    @pl.when(i % 2 == 0)
    def _():
        step(ha_scr, hb_scr)

    @pl.when(i % 2 == 1)
    def _():
        step(hb_scr, ha_scr)


def _proj_call(x2, mod3, norm_g, w_in2, q_g, k_g, lower_bounds, seq_len):
    m, d = x2.shape
    tiles_per_batch = seq_len // PROJ_TM
    n_tiles = m // PROJ_TM
    hpp = PROJ_HEADS_PER_PASS
    grid = (N_HEADS // hpp, n_tiles, hpp)

    def nxt(i):
        return (i + 1) % n_tiles

    def head(c, j):
        return c * hpp + j

    def w_spec(g):
        return pl.BlockSpec(
            (d, HEAD_DIM), lambda c, i, j: (0, g * N_HEADS + head(c, jnp.where(i == 0, j, hpp - 1))))
    return pl.pallas_call(
        _proj_kernel,
        grid=grid,
        in_specs=[
            pl.BlockSpec((PROJ_TM, d), lambda c, i, j: (nxt(i), 0)),
            pl.BlockSpec((PROJ_TM, d), lambda c, i, j: (0, 0), pipeline_mode=pl.Buffered(1)),
            pl.BlockSpec((None, 1, d), lambda c, i, j: (nxt(i) // tiles_per_batch, 0, 0)),
            pl.BlockSpec((None, 1, d), lambda c, i, j: (nxt(i) // tiles_per_batch, 0, 1)),
            pl.BlockSpec((None, 1, d), lambda c, i, j: (0, 0, 0)),
            pl.BlockSpec((None, 1, d), lambda c, i, j: (0, 0, 1)),
            pl.BlockSpec((1, d), lambda c, i, j: (0, 0)),
        ] + [w_spec(g) for g in range(N_GROUPS)] + [
            pl.BlockSpec((1, HEAD_DIM), lambda c, i, j: (0, 0)),
            pl.BlockSpec((1, HEAD_DIM), lambda c, i, j: (0, 0)),
            pl.BlockSpec((lower_bounds.shape[0], HEAD_DIM), lambda c, i, j: (0, head(c, j))),
        ],
        out_specs=[
            pl.BlockSpec((PROJ_TM, QZ_W), lambda c, i, j: (i, head(c, j))),
            pl.BlockSpec((PROJ_TM, QZ_W), lambda c, i, j: (i, head(c, j))),
            pl.BlockSpec((PROJ_TM, HGP_W), lambda c, i, j: (i, head(c, j))),
            pl.BlockSpec((PROJ_TM, HEAD_DIM), lambda c, i, j: (i, head(c, j))),
        ],
        out_shape=[
            jax.ShapeDtypeStruct((m, N_HEADS * QZ_W), BF16),
            jax.ShapeDtypeStruct((m, N_HEADS * QZ_W), BF16),
            jax.ShapeDtypeStruct((m, N_HEADS * HGP_W), BF16),
            jax.ShapeDtypeStruct((m, GROUP), F32),
        ],
        scratch_shapes=[
            pltpu.VMEM((PROJ_TM, d), BF16),
            pltpu.VMEM((PROJ_TM, d), BF16),
            pltpu.VMEM((hpp, d, N_GROUPS * HEAD_DIM), BF16),
        ],
        compiler_params=pltpu.CompilerParams(
            dimension_semantics=("arbitrary", "arbitrary", "arbitrary"),
            vmem_limit_bytes=PROJ_VMEM_LIMIT),
        name="proj",
    )(x2, x2, mod3, mod3, mod3, mod3, norm_g, *([w_in2] * N_GROUPS), q_g, k_g, lower_bounds)


def _build_bias(relb_ref, bias_scr):
    band_w = 4 * CHUNK
    r = lax.broadcasted_iota(jnp.int32, (CHUNK, band_w), 0)
    c = lax.broadcasted_iota(jnp.int32, (CHUNK, band_w), 1)
    dist = 3 * CHUNK + r - c
    n_var = CHUNK - 1 + MAX_REL
    far_w = (LEFT_CHUNKS - 3) * CHUNK
    for h in range(N_HEADS):
        far = relb_ref[h, 2 * MAX_REL]

        def body(d, acc, h=h):
            return jnp.where(dist == d - (CHUNK - 1), relb_ref[h, d + MAX_REL - (CHUNK - 1)], acc)
        near = lax.fori_loop(0, n_var, body, jnp.full((CHUNK, band_w), far, F32))
        for cq in range(ATT_RQ // CHUNK):
            rows = slice(cq * CHUNK, (cq + 1) * CHUNK)
            lo = cq * CHUNK
            bias_scr[h, rows, :] = jnp.full((CHUNK, ATT_WK), -jnp.inf, F32)
            bias_scr[h, rows, lo:lo + far_w] = jnp.full((CHUNK, far_w), far, F32)
            bias_scr[h, rows, lo + far_w:lo + far_w + band_w] = near


def _att_kernel(relb_ref, qz_ref, kv0_ref, kv1_ref, kv2_ref, o_ref, bias_scr):
    b = pl.program_id(0)
    i = pl.program_id(1)

    @pl.when((b == 0) & (i == 0))
    def _():
        _build_bias(relb_ref, bias_scr)

    kpos = lax.broadcasted_iota(jnp.int32, (ATT_RQ, ATT_WK), 1)
    valid = kpos >= (LEFT_CHUNKS * CHUNK - ATT_RQ * i)
    nt = (((1,), (1,)), ((), ()))
    for h in range(N_HEADS):
        qs = slice(h * QZ_W, h * QZ_W + HEAD_DIM)
        zs = slice(h * QZ_W + HEAD_DIM, (h + 1) * QZ_W)
        q = qz_ref[:, qs]
        s = jnp.concatenate(
            [lax.dot_general(q, kr[:, qs], nt, preferred_element_type=F32)
             for kr in (kv0_ref, kv1_ref, kv2_ref)], axis=1)
        s = jnp.where(valid, s + bias_scr[h], -jnp.inf)
        m = jnp.max(s, axis=-1, keepdims=True)
        p = jnp.exp(s - m)
        l = jnp.sum(p, axis=-1, keepdims=True)
        pb = p.astype(BF16)
        o = jnp.dot(pb[:, 0:ATT_RQ], kv0_ref[:, zs], preferred_element_type=F32)
        o += jnp.dot(pb[:, ATT_RQ:2 * ATT_RQ], kv1_ref[:, zs], preferred_element_type=F32)
        o += jnp.dot(pb[:, 2 * ATT_RQ:3 * ATT_RQ], kv2_ref[:, zs], preferred_element_type=F32)
        o = (o / l) * qz_ref[:, zs].astype(F32)
        o_ref[:, _lanes(h)] = o.astype(BF16)


def _att_call(rel_bias, qz, kv, batch, seq_len):
    nq = seq_len // ATT_RQ
    m, w = qz.shape

    def kv_spec(back):
        return pl.BlockSpec((ATT_RQ, w), lambda b, i: (b * nq + jnp.maximum(i - back, 0), 0))
    return pl.pallas_call(
        _att_kernel,
        grid=(batch, nq),
        in_specs=[
            pl.BlockSpec(memory_space=pltpu.SMEM),
            pl.BlockSpec((ATT_RQ, w), lambda b, i: (b * nq + i, 0)),
            kv_spec(2), kv_spec(1), kv_spec(0),
        ],
        out_specs=pl.BlockSpec((ATT_RQ, GROUP), lambda b, i: (b * nq + i, 0)),
        out_shape=jax.ShapeDtypeStruct((m, GROUP), BF16),
        scratch_shapes=[pltpu.VMEM((N_HEADS, ATT_RQ, ATT_WK), F32)],
        compiler_params=pltpu.CompilerParams(
            dimension_semantics=("arbitrary", "arbitrary"), vmem_limit_bytes=VMEM_LIMIT),
        name="att",
    )(rel_bias, qz, kv, kv, kv)


def _hg_piece(h, k):
    return slice(h * HGP_W + k * HEAD_DIM, h * HGP_W + (k + 1) * HEAD_DIM)


def _per_chunk_rows(rows, width):
    return jnp.concatenate([jnp.broadcast_to(r, (CHUNK, width)) for r in rows], axis=0)


def _hg_kernel(p_ref, logf_ref, og_ref, o_ref,
               st_scr, s0_scr, qf_scr, kf_scr, vf_scr, bf_scr, acc_scr):
    t = pl.program_id(1)

    @pl.when(t == 0)
    def _():
        st_scr[...] = jnp.zeros_like(st_scr)

    rb = HG_RB
    c = CHUNK
    row_i = lax.broadcasted_iota(jnp.int32, (rb, rb), 0)
    col_i = lax.broadcasted_iota(jnp.int32, (rb, rb), 1)
    tril = row_i >= col_i
    tri_bf = jnp.where(tril, 1.0, 0.0).astype(BF16)
    same_chunk_tril = tril & ((row_i // c) == (col_i // c))
    nt = (((1,), (1,)), ((), ()))
    tn = (((0,), (0,)), ((), ()))
    og = og_ref[...]

    lf = logf_ref[...]
    hi = lf.astype(BF16)
    r1 = lf - hi.astype(F32)
    md = r1.astype(BF16)
    lo = (r1 - md.astype(F32)).astype(BF16)
    beta = (jnp.dot(tri_bf, lo, preferred_element_type=F32)
            + jnp.dot(tri_bf, md, preferred_element_type=F32)
            + jnp.dot(tri_bf, hi, preferred_element_type=F32))

    mids = [beta[k * c + c // 2 - 1:k * c + c // 2, :] for k in range(4)]
    ends = [beta[k * c + c - 1:k * c + c, :] for k in range(4)]
    x0 = beta - _per_chunk_rows(mids, GROUP)
    e1 = jnp.exp(-jnp.abs(beta - _per_chunk_rows([ends[0], ends[0], ends[2], ends[2]], GROUP)))
    e2 = jnp.exp(-jnp.abs(beta - ends[1]))
    eq = jnp.exp(beta)
    ek = jnp.exp(ends[3] - beta)
    g_blk = jnp.exp(ends[3])
    edge = jnp.concatenate([x0[k * c:k * c + 1, :] for k in range(4)]
                           + [x0[k * c + c - 1:k * c + c, :] for k in range(4)], axis=0)
    steep = jnp.max(jnp.abs(edge)) > EXP_CLAMP

    z64 = jnp.zeros((c, HEAD_DIM), BF16)

    def cross_scores(qt, kh, hs):
        def scaled(v, e, lo_row, hi_row):
            return (v[lo_row:hi_row] * e[lo_row:hi_row, hs]).astype(BF16)
        lhs = jnp.concatenate([
            jnp.concatenate([z64, scaled(qt, e1, c, 2 * c), z64, z64], axis=0),
            jnp.concatenate([z64, z64, z64, scaled(qt, e1, 3 * c, 4 * c)], axis=0),
            jnp.concatenate([z64, z64, scaled(qt, e2, 2 * c, 4 * c)], axis=0)], axis=1)
        rhs = jnp.concatenate([
            jnp.concatenate([scaled(kh, e1, 0, c), z64, z64, z64], axis=0),
            jnp.concatenate([z64, z64, scaled(kh, e1, 2 * c, 3 * c), z64], axis=0),
            jnp.concatenate([scaled(kh, e2, 0, 2 * c), z64, z64], axis=0)], axis=1)
        return lax.dot_general(lhs, rhs, nt, preferred_element_type=F32)

    def carried(qt, st, hs):
        qb = (qt * eq[:, hs]).astype(BF16)
        return lax.dot_general(qb, st.astype(BF16), nt, preferred_element_type=F32)

    def write_out(h, o):
        ms = jnp.mean(o * o, axis=-1, keepdims=True)
        y = o * lax.rsqrt(ms + EPS) * og
        o_ref[:, _lanes(h)] = (y * p_ref[:, _hg_piece(h, 3)].astype(F32)).astype(BF16)

    for h in range(N_HEADS):
        hs = _lanes(h)
        qt = p_ref[:, _hg_piece(h, 0)].astype(F32)
        kh = p_ref[:, _hg_piece(h, 1)].astype(F32)
        vh = p_ref[:, _hg_piece(h, 2)]
        st = st_scr[h]
        s0_scr[h] = st
        xh = x0[:, hs]
        q0 = (qt * jnp.exp(jnp.minimum(xh, EXP_CLAMP))).astype(BF16)
        k0 = (kh * jnp.exp(jnp.minimum(-xh, EXP_CLAMP))).astype(BF16)
        a0 = lax.dot_general(q0, k0, nt, preferred_element_type=F32)
        a = jnp.where(same_chunk_tril, a0, cross_scores(qt, kh, hs)).astype(BF16)
        o = jnp.dot(a, vh, preferred_element_type=F32) + carried(qt, st, hs)
        kd = (kh * ek[:, hs]).astype(BF16)
        st_scr[h] = st * g_blk[:, hs] + lax.dot_general(vh, kd, tn, preferred_element_type=F32)
        write_out(h, o)

    @pl.when(steep)
    def _():
        for h in range(N_HEADS):
            qf_scr[:, _lanes(h)] = p_ref[:, _hg_piece(h, 0)].astype(F32)
            kf_scr[:, _lanes(h)] = p_ref[:, _hg_piece(h, 1)].astype(F32)
            vf_scr[:, _lanes(h)] = p_ref[:, _hg_piece(h, 2)].astype(F32)
        bf_scr[...] = beta
        acc_scr[...] = jnp.zeros_like(acc_scr)
        t_idx = lax.broadcasted_iota(jnp.int32, (c, HEAD_DIM), 0)

        def pair_body(s, carry):
            base = pl.multiple_of((s // c) * c, c)
            rows = pl.ds(base, c)
            k_all = kf_scr[pl.ds(s, 1), :]
            v_all = vf_scr[pl.ds(s, 1), :]
            b_all = bf_scr[pl.ds(s, 1), :]
            for h in range(N_HEADS):
                hs = _lanes(h)
                w = jnp.exp(jnp.minimum(bf_scr[rows, hs] - b_all[:, hs], 0.0))
                p = jnp.where(t_idx >= s - base, qf_scr[rows, hs] * k_all[:, hs] * w, 0.0)
                acc_scr[rows, hs] += jnp.sum(p, axis=-1, keepdims=True) * v_all[:, hs]
            return carry
        lax.fori_loop(0, rb, pair_body, 0)

        for h in range(N_HEADS):
            hs = _lanes(h)
            qt = p_ref[:, _hg_piece(h, 0)].astype(F32)
            kh = p_ref[:, _hg_piece(h, 1)].astype(F32)
            vh = p_ref[:, _hg_piece(h, 2)]
            a = cross_scores(qt, kh, hs).astype(BF16)
            o = (jnp.dot(a, vh, preferred_element_type=F32) + carried(qt, s0_scr[h], hs)
                 + acc_scr[:, hs])
            write_out(h, o)


def _hg_call(hgp, logf, o_g, batch, seq_len):
    nt_ = seq_len // HG_RB
    m, w = hgp.shape
    return pl.pallas_call(
        _hg_kernel,
        grid=(batch, nt_),
        in_specs=[
            pl.BlockSpec((HG_RB, w), lambda b, t: (b * nt_ + t, 0)),
            pl.BlockSpec((HG_RB, GROUP), lambda b, t: (b * nt_ + t, 0)),
            pl.BlockSpec((1, HEAD_DIM), lambda b, t: (0, 0)),
        ],
        out_specs=pl.BlockSpec((HG_RB, GROUP), lambda b, t: (b * nt_ + t, 0)),
        out_shape=jax.ShapeDtypeStruct((m, GROUP), BF16),
        scratch_shapes=[
            pltpu.VMEM((N_HEADS, HEAD_DIM, HEAD_DIM), F32),
            pltpu.VMEM((N_HEADS, HEAD_DIM, HEAD_DIM), F32),
        ] + [pltpu.VMEM((HG_RB, GROUP), F32)] * 5,
        compiler_params=pltpu.CompilerParams(
            dimension_semantics=("arbitrary", "arbitrary"), vmem_limit_bytes=VMEM_LIMIT),
        name="hgrn",
    )(hgp, logf, o_g)


def _out_kernel(att_ref, hg_ref, w_ref, x_ref, gate_ref, o_ref, w_scr):
    @pl.when(pl.program_id(0) == 0)
    def _():
        w_scr[...] = w_ref[...].astype(BF16)

    y = jnp.dot(att_ref[...], w_scr[0:GROUP, :], preferred_element_type=F32)
    y += jnp.dot(hg_ref[...], w_scr[GROUP:2 * GROUP, :], preferred_element_type=F32)
    o_ref[...] = x_ref[...] + gate_ref[...] * y


def _out_call(att, hg, w_out2, x2, mod3, seq_len):
    m, d = x2.shape
    tiles_per_batch = seq_len // OUT_TM
    return pl.pallas_call(
        _out_kernel,
        grid=(m // OUT_TM,),
        in_specs=[
            pl.BlockSpec((OUT_TM, GROUP), lambda i: (i, 0)),
            pl.BlockSpec((OUT_TM, GROUP), lambda i: (i, 0)),
            pl.BlockSpec(w_out2.shape, lambda i: (0, 0), pipeline_mode=pl.Buffered(1)),
            pl.BlockSpec((OUT_TM, d), lambda i: (i, 0)),
            pl.BlockSpec((None, 1, d), lambda i: (i // tiles_per_batch, 0, 2)),
        ],
        out_specs=pl.BlockSpec((OUT_TM, d), lambda i: (i, 0)),
        out_shape=jax.ShapeDtypeStruct((m, d), F32),
        scratch_shapes=[pltpu.VMEM(w_out2.shape, BF16)],
        compiler_params=pltpu.CompilerParams(
            dimension_semantics=("arbitrary",), vmem_limit_bytes=VMEM_LIMIT),
        name="outproj",
    )(att, hg, w_out2, x2, mod3)


def kernel(x, c, norm_g, w_ada, b_ada, w_in, q_norm_g, k_norm_g, rel_bias,
           lower_bounds, hg_norm_g, w_out):
    batch, seq_len, d = x.shape
    assert w_in.shape == (1, d, N_GROUPS * GROUP) and w_out.shape == (1, 2 * GROUP, d)
    assert seq_len % PROJ_TM == 0 and seq_len % ATT_RQ == 0 and seq_len % HG_RB == 0
    assert rel_bias.shape == (1, N_HEADS, 2 * MAX_REL + 1)

    x2 = x.reshape(batch * seq_len, d)
    c_pad = jnp.pad(c, ((0, 8 - batch), (0, 0)))
    mod = _ada_call(c_pad, w_ada, b_ada)
    mod3 = mod[:batch].reshape(batch, 1, 3 * d)

    qz, kv, hgp, logf = _proj_call(x2, mod3, norm_g, w_in[0], q_norm_g, k_norm_g,
                                   lower_bounds, seq_len)
    att = _att_call(rel_bias[0], qz, kv, batch, seq_len)
    hg = _hg_call(hgp, logf, hg_norm_g, batch, seq_len)
    out = _out_call(att, hg, w_out[0], x2, mod3, seq_len)
    return out.reshape(batch, seq_len, d)
```

```python
import jax
import jax.numpy as jnp
from jax import lax
from jax.experimental import pallas as pl
from jax.experimental.pallas import tpu as pltpu

F32 = jnp.float32
BF16 = jnp.bfloat16

CHUNK = 64
HEAD_DIM = 128
N_HEADS = 8
GROUP = N_HEADS * HEAD_DIM
N_GROUPS = 8
LEFT_CHUNKS = 8
MAX_REL = 128
EPS = 1e-6
EXP_CLAMP = 80.0

VMEM_LIMIT = 56 * 1024 * 1024

ADA_TN = 768
PROJ_TM = 512
PROJ_HEADS_PER_PASS = 4
PROJ_VMEM_LIMIT = 60 * 1024 * 1024
PROJ_COL_ORDER = (5, 4, 0, 1, 7, 3, 2, 6)
ATT_RQ = 256
ATT_WK = ATT_RQ + LEFT_CHUNKS * CHUNK
ATT_MASK_VARIANTS = LEFT_CHUNKS * CHUNK // ATT_RQ + 1
LOG2E = 1.4426950408889634
Q_SCALE = HEAD_DIM ** -0.5 * LOG2E
HG_RB = 256
OUT_TM = 512

QZ_W = 2 * HEAD_DIM
HGP_W = 4 * HEAD_DIM


def _sigmoid(v):
    return 0.5 * jnp.tanh(0.5 * v) + 0.5


def _silu(v):
    return v * _sigmoid(v)


def _lanes(piece, width=HEAD_DIM):
    return slice(piece * width, (piece + 1) * width)


def _ada_kernel(c_ref, w_ref, b_ref, o_ref):
    ca = _silu(c_ref[...])
    o_ref[...] = jnp.dot(ca, w_ref[...], preferred_element_type=F32,
                         precision=lax.Precision.HIGHEST) + b_ref[...]


def _ada_call(c_pad, w_ada, b_ada):
    rows, d = c_pad.shape
    n = w_ada.shape[-1]
    return pl.pallas_call(
        _ada_kernel,
        grid=(n // ADA_TN,),
        in_specs=[
            pl.BlockSpec((rows, d), lambda j: (0, 0)),
            pl.BlockSpec((None, d, ADA_TN), lambda j: (0, 0, j)),
            pl.BlockSpec((1, ADA_TN), lambda j: (0, j)),
        ],
        out_specs=pl.BlockSpec((rows, ADA_TN), lambda j: (0, j)),
        out_shape=jax.ShapeDtypeStruct((rows, n), F32),
        compiler_params=pltpu.CompilerParams(
            dimension_semantics=("arbitrary",), vmem_limit_bytes=VMEM_LIMIT),
        name="ada",
    )(c_pad, w_ada, b_ada)


def _proj_kernel(xn_ref, x0_ref, shn_ref, scn_ref, sh0_ref, sc0_ref, ng_ref,
                 w0, w1, w2, w3, w4, w5, w6, w7,
                 qg_ref, kg_ref, lb_ref,
                 qz_ref, kv_ref, hgp_ref, logf_ref, ha_scr, hb_scr, w_scr):
    c = pl.program_id(0)
    i = pl.program_id(1)
    j = pl.program_id(2)
    tm = xn_ref.shape[0]
    slab = tm // PROJ_HEADS_PER_PASS
    gain = ng_ref[...]

    def normed(xv, sc_ref, sh_ref):
        ms = jnp.mean(xv * xv, axis=-1, keepdims=True)
        y = xv * lax.rsqrt(ms + EPS) * gain
        return (y * (1.0 + sc_ref[...]) + sh_ref[...]).astype(BF16)

    @pl.when((c == 0) & (i == 0) & (j == 0))
    def _():
        def body(r, carry):
            rows = pl.ds(pl.multiple_of(r * slab, slab), slab)
            ha_scr[rows, :] = normed(x0_ref[rows, :], sc0_ref, sh0_ref)
            return carry
        lax.fori_loop(0, PROJ_HEADS_PER_PASS, body, 0)

    @pl.when(i == 0)
    def _():
        w_refs = (w0, w1, w2, w3, w4, w5, w6, w7)
        for pos, g in enumerate(PROJ_COL_ORDER):
            w_scr[j, :, _lanes(pos)] = w_refs[g][...].astype(BF16)

    def head_rms(a, gain):
        ms = jnp.mean(a * a, axis=-1, keepdims=True)
        return a * lax.rsqrt(ms + EPS) * gain

    def step(h_cur, h_nxt):
        rows = pl.ds(pl.multiple_of(j * slab, slab), slab)
        h_nxt[rows, :] = normed(xn_ref[rows, :], scn_ref, shn_ref)
        acc = jnp.dot(h_cur[...], w_scr[j], preferred_element_type=F32)

        def group(g):
            return acc[:, _lanes(PROJ_COL_ORDER.index(g))]
        lw = lb_ref[...]
        e = jnp.exp(lw - jnp.max(lw, axis=0, keepdims=True))
        lb = e[0:1, :] / jnp.sum(e, axis=0, keepdims=True)
        sg = _sigmoid(group(5))
        logf_ref[...] = jnp.log(lb + (1.0 - lb) * sg)
        hgp_ref[:, _lanes(1)] = ((1.0 - lb) * (1.0 - sg)).astype(BF16)
        hgp_ref[:, _lanes(0)] = _silu(group(4)).astype(BF16)
        hgp_ref[:, _lanes(3)] = _silu(group(7)).astype(BF16)
        hgp_ref[:, _lanes(2)] = group(6).astype(BF16)
        qz_ref[:, _lanes(1)] = _silu(group(3)).astype(BF16)
        qz_ref[:, _lanes(0)] = (head_rms(group(0), qg_ref[...]) * Q_SCALE).astype(BF16)
        kv_ref[:, _lanes(0)] = head_rms(group(1), kg_ref[...]).astype(BF16)
        kv_ref[:, _lanes(1)] = group(2).astype(BF16)

    @pl.when(i % 2 == 0)
    def _():
        step(ha_scr, hb_scr)

    @pl.when(i % 2 == 1)
    def _():
        step(hb_scr, ha_scr)


def _proj_call(x2, mod3, norm_g, w_in2, q_g, k_g, lower_bounds, seq_len):
    m, d = x2.shape
    tiles_per_batch = seq_len // PROJ_TM
    n_tiles = m // PROJ_TM
    hpp = PROJ_HEADS_PER_PASS
    grid = (N_HEADS // hpp, n_tiles, hpp)

    def nxt(i):
        return (i + 1) % n_tiles

    def head(c, j):
        return c * hpp + j

    def w_spec(g):
        return pl.BlockSpec(
            (d, HEAD_DIM), lambda c, i, j: (0, g * N_HEADS + head(c, jnp.where(i == 0, j, hpp - 1))))
    return pl.pallas_call(
        _proj_kernel,
        grid=grid,
        in_specs=[
            pl.BlockSpec((PROJ_TM, d), lambda c, i, j: (nxt(i), 0)),
            pl.BlockSpec((PROJ_TM, d), lambda c, i, j: (0, 0), pipeline_mode=pl.Buffered(1)),
            pl.BlockSpec((None, 1, d), lambda c, i, j: (nxt(i) // tiles_per_batch, 0, 0)),
            pl.BlockSpec((None, 1, d), lambda c, i, j: (nxt(i) // tiles_per_batch, 0, 1)),
            pl.BlockSpec((None, 1, d), lambda c, i, j: (0, 0, 0)),
            pl.BlockSpec((None, 1, d), lambda c, i, j: (0, 0, 1)),
            pl.BlockSpec((1, d), lambda c, i, j: (0, 0)),
        ] + [w_spec(g) for g in range(N_GROUPS)] + [
            pl.BlockSpec((1, HEAD_DIM), lambda c, i, j: (0, 0)),
            pl.BlockSpec((1, HEAD_DIM), lambda c, i, j: (0, 0)),
            pl.BlockSpec((lower_bounds.shape[0], HEAD_DIM), lambda c, i, j: (0, head(c, j))),
        ],
        out_specs=[
            pl.BlockSpec((PROJ_TM, QZ_W), lambda c, i, j: (i, head(c, j))),
            pl.BlockSpec((PROJ_TM, QZ_W), lambda c, i, j: (i, head(c, j))),
            pl.BlockSpec((PROJ_TM, HGP_W), lambda c, i, j: (i, head(c, j))),
            pl.BlockSpec((PROJ_TM, HEAD_DIM), lambda c, i, j: (i, head(c, j))),
        ],
        out_shape=[
            jax.ShapeDtypeStruct((m, N_HEADS * QZ_W), BF16),
            jax.ShapeDtypeStruct((m, N_HEADS * QZ_W), BF16),
            jax.ShapeDtypeStruct((m, N_HEADS * HGP_W), BF16),
            jax.ShapeDtypeStruct((m, GROUP), F32),
        ],
        scratch_shapes=[
            pltpu.VMEM((PROJ_TM, d), BF16),
            pltpu.VMEM((PROJ_TM, d), BF16),
            pltpu.VMEM((hpp, d, N_GROUPS * HEAD_DIM), BF16),
        ],
        compiler_params=pltpu.CompilerParams(
            dimension_semantics=("arbitrary", "arbitrary", "arbitrary"),
            vmem_limit_bytes=PROJ_VMEM_LIMIT),
        name="proj",
    )(x2, x2, mod3, mod3, mod3, mod3, norm_g, *([w_in2] * N_GROUPS), q_g, k_g, lower_bounds)


def _build_bias(relb_ref, bias_scr):
    band_w = 4 * CHUNK
    r = lax.broadcasted_iota(jnp.int32, (CHUNK, band_w), 0)
    c = lax.broadcasted_iota(jnp.int32, (CHUNK, band_w), 1)
    dist = 3 * CHUNK + r - c
    n_var = CHUNK - 1 + MAX_REL
    far_w = (LEFT_CHUNKS - 3) * CHUNK
    full = ATT_MASK_VARIANTS - 1
    kpos = lax.broadcasted_iota(jnp.int32, (ATT_RQ, ATT_WK), 1)
    for h in range(N_HEADS):
        far = relb_ref[h, 2 * MAX_REL] * LOG2E

        def body(d, acc, h=h):
            val = relb_ref[h, d + MAX_REL - (CHUNK - 1)] * LOG2E
            return jnp.where(dist == d - (CHUNK - 1), val, acc)
        near = lax.fori_loop(0, n_var, body, jnp.full((CHUNK, band_w), far, F32))
        for cq in range(ATT_RQ // CHUNK):
            rows = slice(cq * CHUNK, (cq + 1) * CHUNK)
            lo = cq * CHUNK
            bias_scr[full, h, rows, :] = jnp.full((CHUNK, ATT_WK), -jnp.inf, F32)
            bias_scr[full, h, rows, lo:lo + far_w] = jnp.full((CHUNK, far_w), far, F32)
            bias_scr[full, h, rows, lo + far_w:lo + far_w + band_w] = near
        for v in range(full):
            first_valid = LEFT_CHUNKS * CHUNK - ATT_RQ * v
            bias_scr[v, h] = jnp.where(kpos >= first_valid, bias_scr[full, h], -jnp.inf)


def _att_kernel(relb_ref, qz_ref, kv0_ref, kv1_ref, kv2_ref, o_ref, bias_scr):
    b = pl.program_id(0)
    i = pl.program_id(1)

    @pl.when((b == 0) & (i == 0))
    def _():
        _build_bias(relb_ref, bias_scr)

    variant = jnp.minimum(i, ATT_MASK_VARIANTS - 1)
    nt = (((1,), (1,)), ((), ()))
    ones = jnp.ones((ATT_RQ, HEAD_DIM), BF16)
    for h in range(N_HEADS):
        qs = slice(h * QZ_W, h * QZ_W + HEAD_DIM)
        zs = slice(h * QZ_W + HEAD_DIM, (h + 1) * QZ_W)
        q = qz_ref[:, qs]
        s = jnp.concatenate(
            [lax.dot_general(q, kr[:, qs], nt, preferred_element_type=F32)
             for kr in (kv0_ref, kv1_ref, kv2_ref)], axis=1)
        s = s + bias_scr[variant, h]
        m = jnp.max(s, axis=-1, keepdims=True)
        pb = jnp.exp2(s - m).astype(BF16)
        ol = None
        for blk, kr in enumerate((kv0_ref, kv1_ref, kv2_ref)):
            v_aug = jnp.concatenate([kr[:, zs], ones], axis=1)
            part = jnp.dot(pb[:, blk * ATT_RQ:(blk + 1) * ATT_RQ], v_aug,
                           preferred_element_type=F32)
            ol = part if ol is None else ol + part
        o = (ol[:, 0:HEAD_DIM] / ol[:, HEAD_DIM:2 * HEAD_DIM]) * qz_ref[:, zs].astype(F32)
        o_ref[:, _lanes(h)] = o.astype(BF16)


def _att_call(rel_bias, qz, kv, batch, seq_len):
    nq = seq_len // ATT_RQ
    m, w = qz.shape

    def kv_spec(back):
        return pl.BlockSpec((ATT_RQ, w), lambda b, i: (b * nq + jnp.maximum(i - back, 0), 0))
    return pl.pallas_call(
        _att_kernel,
        grid=(batch, nq),
        in_specs=[
            pl.BlockSpec(memory_space=pltpu.SMEM),
            pl.BlockSpec((ATT_RQ, w), lambda b, i: (b * nq + i, 0)),
            kv_spec(2), kv_spec(1), kv_spec(0),
        ],
        out_specs=pl.BlockSpec((ATT_RQ, GROUP), lambda b, i: (b * nq + i, 0)),
        out_shape=jax.ShapeDtypeStruct((m, GROUP), BF16),
        scratch_shapes=[pltpu.VMEM((ATT_MASK_VARIANTS, N_HEADS, ATT_RQ, ATT_WK), F32)],
        compiler_params=pltpu.CompilerParams(
            dimension_semantics=("arbitrary", "arbitrary"), vmem_limit_bytes=VMEM_LIMIT),
        name="att",
    )(rel_bias, qz, kv, kv, kv)


def _hg_piece(h, k):
    return slice(h * HGP_W + k * HEAD_DIM, h * HGP_W + (k + 1) * HEAD_DIM)


def _per_chunk_rows(rows, width):
    return jnp.concatenate([jnp.broadcast_to(r, (CHUNK, width)) for r in rows], axis=0)


def _hg_kernel(p_ref, logf_ref, og_ref, o_ref,
               st_scr, s0_scr, qf_scr, kf_scr, vf_scr, bf_scr, acc_scr):
    t = pl.program_id(1)

    @pl.when(t == 0)
    def _():
        st_scr[...] = jnp.zeros_like(st_scr)

    rb = HG_RB
    c = CHUNK
    row_i = lax.broadcasted_iota(jnp.int32, (rb, rb), 0)
    col_i = lax.broadcasted_iota(jnp.int32, (rb, rb), 1)
    tril = row_i >= col_i
    tri_bf = jnp.where(tril, 1.0, 0.0).astype(BF16)
    same_chunk_tril = tril & ((row_i // c) == (col_i // c))
    nt = (((1,), (1,)), ((), ()))
    tn = (((0,), (0,)), ((), ()))
    og = og_ref[...]

    lf = logf_ref[...]
    hi = lf.astype(BF16)
    r1 = lf - hi.astype(F32)
    md = r1.astype(BF16)
    lo = (r1 - md.astype(F32)).astype(BF16)
    beta = (jnp.dot(tri_bf, lo, preferred_element_type=F32)
            + jnp.dot(tri_bf, md, preferred_element_type=F32)
            + jnp.dot(tri_bf, hi, preferred_element_type=F32))

    mids = [beta[k * c + c // 2 - 1:k * c + c // 2, :] for k in range(4)]
    ends = [beta[k * c + c - 1:k * c + c, :] for k in range(4)]
    x0 = beta - _per_chunk_rows(mids, GROUP)
    e1 = jnp.exp(-jnp.abs(beta - _per_chunk_rows([ends[0], ends[0], ends[2], ends[2]], GROUP)))
    e2 = jnp.exp(-jnp.abs(beta - ends[1]))
    eq = jnp.exp(beta)
    ek = jnp.exp(ends[3] - beta)
    g_blk = jnp.exp(ends[3])
    edge = jnp.concatenate([x0[k * c:k * c + 1, :] for k in range(4)]
                           + [x0[k * c + c - 1:k * c + c, :] for k in range(4)], axis=0)
    steep = jnp.max(jnp.abs(edge)) > EXP_CLAMP

    z64 = jnp.zeros((c, HEAD_DIM), BF16)

    def cross_scores(qt, kh, hs):
        def scaled(v, e, lo_row, hi_row):
            return (v[lo_row:hi_row] * e[lo_row:hi_row, hs]).astype(BF16)
        lhs = jnp.concatenate([
            jnp.concatenate([z64, scaled(qt, e1, c, 2 * c), z64, z64], axis=0),
            jnp.concatenate([z64, z64, z64, scaled(qt, e1, 3 * c, 4 * c)], axis=0),
            jnp.concatenate([z64, z64, scaled(qt, e2, 2 * c, 4 * c)], axis=0)], axis=1)
        rhs = jnp.concatenate([
            jnp.concatenate([scaled(kh, e1, 0, c), z64, z64, z64], axis=0),
            jnp.concatenate([z64, z64, scaled(kh, e1, 2 * c, 3 * c), z64], axis=0),
            jnp.concatenate([scaled(kh, e2, 0, 2 * c), z64, z64], axis=0)], axis=1)
        return lax.dot_general(lhs, rhs, nt, preferred_element_type=F32)

    def carried(qt, st, hs):
        qb = (qt * eq[:, hs]).astype(BF16)
        return lax.dot_general(qb, st.astype(BF16), nt, preferred_element_type=F32)

    def write_out(h, o):
        ms = jnp.mean(o * o, axis=-1, keepdims=True)
        y = o * lax.rsqrt(ms + EPS) * og
        o_ref[:, _lanes(h)] = (y * p_ref[:, _hg_piece(h, 3)].astype(F32)).astype(BF16)

    for h in range(N_HEADS):
        hs = _lanes(h)
        qt = p_ref[:, _hg_piece(h, 0)].astype(F32)
        kh = p_ref[:, _hg_piece(h, 1)].astype(F32)
        vh = p_ref[:, _hg_piece(h, 2)]
        st = st_scr[h]
        s0_scr[h] = st
        xh = x0[:, hs]
        q0 = (qt * jnp.exp(jnp.minimum(xh, EXP_CLAMP))).astype(BF16)
        k0 = (kh * jnp.exp(jnp.minimum(-xh, EXP_CLAMP))).astype(BF16)
        a0 = lax.dot_general(q0, k0, nt, preferred_element_type=F32)
        a = jnp.where(same_chunk_tril, a0, cross_scores(qt, kh, hs)).astype(BF16)
        o = jnp.dot(a, vh, preferred_element_type=F32) + carried(qt, st, hs)
        kd = (kh * ek[:, hs]).astype(BF16)
        st_scr[h] = st * g_blk[:, hs] + lax.dot_general(vh, kd, tn, preferred_element_type=F32)
        write_out(h, o)

    @pl.when(steep)
    def _():
        for h in range(N_HEADS):
            qf_scr[:, _lanes(h)] = p_ref[:, _hg_piece(h, 0)].astype(F32)
            kf_scr[:, _lanes(h)] = p_ref[:, _hg_piece(h, 1)].astype(F32)
            vf_scr[:, _lanes(h)] = p_ref[:, _hg_piece(h, 2)].astype(F32)
        bf_scr[...] = beta
        acc_scr[...] = jnp.zeros_like(acc_scr)
        t_idx = lax.broadcasted_iota(jnp.int32, (c, HEAD_DIM), 0)

        def pair_body(s, carry):
            base = pl.multiple_of((s // c) * c, c)
            rows = pl.ds(base, c)
            k_all = kf_scr[pl.ds(s, 1), :]
            v_all = vf_scr[pl.ds(s, 1), :]
            b_all = bf_scr[pl.ds(s, 1), :]
            for h in range(N_HEADS):
                hs = _lanes(h)
                w = jnp.exp(jnp.minimum(bf_scr[rows, hs] - b_all[:, hs], 0.0))
                p = jnp.where(t_idx >= s - base, qf_scr[rows, hs] * k_all[:, hs] * w, 0.0)
                acc_scr[rows, hs] += jnp.sum(p, axis=-1, keepdims=True) * v_all[:, hs]
            return carry
        lax.fori_loop(0, rb, pair_body, 0)

        for h in range(N_HEADS):
            hs = _lanes(h)
            qt = p_ref[:, _hg_piece(h, 0)].astype(F32)
            kh = p_ref[:, _hg_piece(h, 1)].astype(F32)
            vh = p_ref[:, _hg_piece(h, 2)]
            a = cross_scores(qt, kh, hs).astype(BF16)
            o = (jnp.dot(a, vh, preferred_element_type=F32) + carried(qt, s0_scr[h], hs)
                 + acc_scr[:, hs])
            write_out(h, o)


def _hg_call(hgp, logf, o_g, batch, seq_len):
    nt_ = seq_len // HG_RB
    m, w = hgp.shape
    return pl.pallas_call(
        _hg_kernel,
        grid=(batch, nt_),
        in_specs=[
            pl.BlockSpec((HG_RB, w), lambda b, t: (b * nt_ + t, 0)),
            pl.BlockSpec((HG_RB, GROUP), lambda b, t: (b * nt_ + t, 0)),
            pl.BlockSpec((1, HEAD_DIM), lambda b, t: (0, 0)),
        ],
        out_specs=pl.BlockSpec((HG_RB, GROUP), lambda b, t: (b * nt_ + t, 0)),
        out_shape=jax.ShapeDtypeStruct((m, GROUP), BF16),
        scratch_shapes=[
            pltpu.VMEM((N_HEADS, HEAD_DIM, HEAD_DIM), F32),
            pltpu.VMEM((N_HEADS, HEAD_DIM, HEAD_DIM), F32),
        ] + [pltpu.VMEM((HG_RB, GROUP), F32)] * 5,
        compiler_params=pltpu.CompilerParams(
            dimension_semantics=("arbitrary", "arbitrary"), vmem_limit_bytes=VMEM_LIMIT),
        name="hgrn",
    )(hgp, logf, o_g)


def _out_kernel(att_ref, hg_ref, w_ref, x_ref, gate_ref, o_ref, w_scr):
    @pl.when(pl.program_id(0) == 0)
    def _():
        w_scr[...] = w_ref[...].astype(BF16)

    y = jnp.dot(att_ref[...], w_scr[0:GROUP, :], preferred_element_type=F32)
    y += jnp.dot(hg_ref[...], w_scr[GROUP:2 * GROUP, :], preferred_element_type=F32)
    o_ref[...] = x_ref[...] + gate_ref[...] * y


def _out_call(att, hg, w_out2, x2, mod3, seq_len):
    m, d = x2.shape
    tiles_per_batch = seq_len // OUT_TM
    return pl.pallas_call(
        _out_kernel,
        grid=(m // OUT_TM,),
        in_specs=[
            pl.BlockSpec((OUT_TM, GROUP), lambda i: (i, 0)),
            pl.BlockSpec((OUT_TM, GROUP), lambda i: (i, 0)),
            pl.BlockSpec(w_out2.shape, lambda i: (0, 0), pipeline_mode=pl.Buffered(1)),
            pl.BlockSpec((OUT_TM, d), lambda i: (i, 0)),
            pl.BlockSpec((None, 1, d), lambda i: (i // tiles_per_batch, 0, 2)),
        ],
        out_specs=pl.BlockSpec((OUT_TM, d), lambda i: (i, 0)),
        out_shape=jax.ShapeDtypeStruct((m, d), F32),
        scratch_shapes=[pltpu.VMEM(w_out2.shape, BF16)],
        compiler_params=pltpu.CompilerParams(
            dimension_semantics=("arbitrary",), vmem_limit_bytes=VMEM_LIMIT),
        name="outproj",
    )(att, hg, w_out2, x2, mod3)


def kernel(x, c, norm_g, w_ada, b_ada, w_in, q_norm_g, k_norm_g, rel_bias,
           lower_bounds, hg_norm_g, w_out):
    batch, seq_len, d = x.shape
    assert w_in.shape == (1, d, N_GROUPS * GROUP) and w_out.shape == (1, 2 * GROUP, d)
    assert seq_len % PROJ_TM == 0 and seq_len % ATT_RQ == 0 and seq_len % HG_RB == 0
    assert rel_bias.shape == (1, N_HEADS, 2 * MAX_REL + 1)

    x2 = x.reshape(batch * seq_len, d)
    c_pad = jnp.pad(c, ((0, 8 - batch), (0, 0)))
    mod = _ada_call(c_pad, w_ada, b_ada)
    mod3 = mod[:batch].reshape(batch, 1, 3 * d)

    qz, kv, hgp, logf = _proj_call(x2, mod3, norm_g, w_in[0], q_norm_g, k_norm_g,
                                   lower_bounds, seq_len)
    att = _att_call(rel_bias[0], qz, kv, batch, seq_len)
    hg = _hg_call(hgp, logf, hg_norm_g, batch, seq_len)
    out = _out_call(att, hg, w_out[0], x2, mod3, seq_len)
    return out.reshape(batch, seq_len, d)
```

```python
import jax
import jax.numpy as jnp
from jax import lax
from jax.experimental import pallas as pl
from jax.experimental.pallas import tpu as pltpu

F32 = jnp.float32
BF16 = jnp.bfloat16

CHUNK = 64
HEAD_DIM = 128
N_HEADS = 8
GROUP = N_HEADS * HEAD_DIM
N_GROUPS = 8
LEFT_CHUNKS = 8
MAX_REL = 128
EPS = 1e-6
EXP_CLAMP = 80.0

VMEM_LIMIT = 56 * 1024 * 1024

ADA_TN = 768
PROJ_TM = 512
PROJ_HEADS_PER_PASS = 4
PROJ_VMEM_LIMIT = 60 * 1024 * 1024
PROJ_COL_ORDER = (5, 4, 0, 1, 7, 3, 2, 6)
ATT_RQ = 256
ATT_WK = ATT_RQ + LEFT_CHUNKS * CHUNK
ATT_MASK_VARIANTS = LEFT_CHUNKS * CHUNK // ATT_RQ + 1
LOG2E = 1.4426950408889634
Q_SCALE = HEAD_DIM ** -0.5 * LOG2E
HG_RB = 256
OUT_TM = 512

QZ_W = 2 * HEAD_DIM
HGP_W = 4 * HEAD_DIM


def _sigmoid(v):
    return 0.5 * jnp.tanh(0.5 * v) + 0.5


def _silu(v):
    return v * _sigmoid(v)


def _lanes(piece, width=HEAD_DIM):
    return slice(piece * width, (piece + 1) * width)


def _ada_kernel(c_ref, w_ref, b_ref, o_ref, cb_scr):
    batch = cb_scr.shape[0]
    d = w_ref.shape[0]
    tn = w_ref.shape[1]

    @pl.when(pl.program_id(0) == 0)
    def _():
        ca = _silu(c_ref[...])
        ca_t = jnp.concatenate([ca, jnp.zeros((HEAD_DIM - ca.shape[0], d), F32)], axis=0).T
        for b in range(batch):
            cb_scr[b] = jnp.broadcast_to(ca_t[:, b:b + 1], (d, HEAD_DIM))

    def body(kt, accs):
        rows = pl.ds(pl.multiple_of(kt * 8, 8), 8)
        wv = w_ref[rows, :]
        return tuple(acc + jnp.tile(cb_scr[b, rows, :], (1, tn // HEAD_DIM)) * wv
                     for b, acc in enumerate(accs))
    accs = lax.fori_loop(0, d // 8, body, tuple(jnp.zeros((8, tn), F32) for _ in range(batch)),
                         unroll=8)
    out_rows = [jnp.sum(acc, axis=0, keepdims=True) for acc in accs]
    out_rows.append(jnp.zeros((o_ref.shape[0] - batch, tn), F32))
    o_ref[...] = jnp.concatenate(out_rows, axis=0) + b_ref[...]


def _ada_call(c_pad, w_ada, b_ada, batch):
    rows, d = c_pad.shape
    n = w_ada.shape[-1]
    return pl.pallas_call(
        _ada_kernel,
        grid=(n // ADA_TN,),
        in_specs=[
            pl.BlockSpec((rows, d), lambda j: (0, 0)),
            pl.BlockSpec((None, d, ADA_TN), lambda j: (0, 0, j)),
            pl.BlockSpec((1, ADA_TN), lambda j: (0, j)),
        ],
        out_specs=pl.BlockSpec((rows, ADA_TN), lambda j: (0, j)),
        out_shape=jax.ShapeDtypeStruct((rows, n), F32),
        scratch_shapes=[pltpu.VMEM((batch, d, HEAD_DIM), F32)],
        compiler_params=pltpu.CompilerParams(
            dimension_semantics=("arbitrary",), vmem_limit_bytes=VMEM_LIMIT),
        name="ada",
    )(c_pad, w_ada, b_ada)


def _proj_kernel(xn_ref, x0_ref, shn_ref, scn_ref, sh0_ref, sc0_ref, ng_ref,
                 w0, w1, w2, w3, w4, w5, w6, w7,
                 qg_ref, kg_ref, lb_ref,
                 qz_ref, kv_ref, hgp_ref, logf_ref, ha_scr, hb_scr, w_scr):
    c = pl.program_id(0)
    i = pl.program_id(1)
    j = pl.program_id(2)
    tm = xn_ref.shape[0]
    slab = tm // PROJ_HEADS_PER_PASS
    gain = ng_ref[...]

    def normed(xv, sc_ref, sh_ref):
        ms = jnp.mean(xv * xv, axis=-1, keepdims=True)
        y = xv * lax.rsqrt(ms + EPS) * gain
        return (y * (1.0 + sc_ref[...]) + sh_ref[...]).astype(BF16)

    @pl.when((c == 0) & (i == 0) & (j == 0))
    def _():
        def body(r, carry):
            rows = pl.ds(pl.multiple_of(r * slab, slab), slab)
            ha_scr[rows, :] = normed(x0_ref[rows, :], sc0_ref, sh0_ref)
            return carry
        lax.fori_loop(0, PROJ_HEADS_PER_PASS, body, 0)

    @pl.when(i == 0)
    def _():
        w_refs = (w0, w1, w2, w3, w4, w5, w6, w7)
        for pos, g in enumerate(PROJ_COL_ORDER):
            w_scr[j, :, _lanes(pos)] = w_refs[g][...].astype(BF16)

    def head_rms(a, gain):
        ms = jnp.mean(a * a, axis=-1, keepdims=True)
        return a * lax.rsqrt(ms + EPS) * gain

    def step(h_cur, h_nxt):
        rows = pl.ds(pl.multiple_of(j * slab, slab), slab)
        h_nxt[rows, :] = normed(xn_ref[rows, :], scn_ref, shn_ref)
        acc = jnp.dot(h_cur[...], w_scr[j], preferred_element_type=F32)

        def group(g):
            return acc[:, _lanes(PROJ_COL_ORDER.index(g))]
        lw = lb_ref[...]
        e = jnp.exp(lw - jnp.max(lw, axis=0, keepdims=True))
        lb = e[0:1, :] / jnp.sum(e, axis=0, keepdims=True)
        sg = _sigmoid(group(5))
        logf_ref[...] = jnp.log(lb + (1.0 - lb) * sg)
        hgp_ref[:, _lanes(1)] = ((1.0 - lb) * (1.0 - sg)).astype(BF16)
        hgp_ref[:, _lanes(0)] = _silu(group(4)).astype(BF16)
        hgp_ref[:, _lanes(3)] = _silu(group(7)).astype(BF16)
        hgp_ref[:, _lanes(2)] = group(6).astype(BF16)
        qz_ref[:, _lanes(1)] = _silu(group(3)).astype(BF16)
        qz_ref[:, _lanes(0)] = (head_rms(group(0), qg_ref[...]) * Q_SCALE).astype(BF16)
        kv_ref[:, _lanes(0)] = head_rms(group(1), kg_ref[...]).astype(BF16)
        kv_ref[:, _lanes(1)] = group(2).astype(BF16)

    @pl.when(i % 2 == 0)
    def _():
        step(ha_scr, hb_scr)

    @pl.when(i % 2 == 1)
    def _():
        step(hb_scr, ha_scr)


def _proj_call(x2, mod3, norm_g, w_in2, q_g, k_g, lower_bounds, seq_len):
    m, d = x2.shape
    tiles_per_batch = seq_len // PROJ_TM
    n_tiles = m // PROJ_TM
    hpp = PROJ_HEADS_PER_PASS
    grid = (N_HEADS // hpp, n_tiles, hpp)

    def nxt(i):
        return (i + 1) % n_tiles

    def head(c, j):
        return c * hpp + j

    def w_spec(g):
        return pl.BlockSpec(
            (d, HEAD_DIM), lambda c, i, j: (0, g * N_HEADS + head(c, jnp.where(i == 0, j, hpp - 1))))
    return pl.pallas_call(
        _proj_kernel,
        grid=grid,
        in_specs=[
            pl.BlockSpec((PROJ_TM, d), lambda c, i, j: (nxt(i), 0)),
            pl.BlockSpec((PROJ_TM, d), lambda c, i, j: (0, 0), pipeline_mode=pl.Buffered(1)),
            pl.BlockSpec((None, 1, d), lambda c, i, j: (nxt(i) // tiles_per_batch, 0, 0)),
            pl.BlockSpec((None, 1, d), lambda c, i, j: (nxt(i) // tiles_per_batch, 0, 1)),
            pl.BlockSpec((None, 1, d), lambda c, i, j: (0, 0, 0)),
            pl.BlockSpec((None, 1, d), lambda c, i, j: (0, 0, 1)),
            pl.BlockSpec((1, d), lambda c, i, j: (0, 0)),
        ] + [w_spec(g) for g in range(N_GROUPS)] + [
            pl.BlockSpec((1, HEAD_DIM), lambda c, i, j: (0, 0)),
            pl.BlockSpec((1, HEAD_DIM), lambda c, i, j: (0, 0)),
            pl.BlockSpec((lower_bounds.shape[0], HEAD_DIM), lambda c, i, j: (0, head(c, j))),
        ],
        out_specs=[
            pl.BlockSpec((PROJ_TM, QZ_W), lambda c, i, j: (i, head(c, j))),
            pl.BlockSpec((PROJ_TM, QZ_W), lambda c, i, j: (i, head(c, j))),
            pl.BlockSpec((PROJ_TM, HGP_W), lambda c, i, j: (i, head(c, j))),
            pl.BlockSpec((PROJ_TM, HEAD_DIM), lambda c, i, j: (i, head(c, j))),
        ],
        out_shape=[
            jax.ShapeDtypeStruct((m, N_HEADS * QZ_W), BF16),
            jax.ShapeDtypeStruct((m, N_HEADS * QZ_W), BF16),
            jax.ShapeDtypeStruct((m, N_HEADS * HGP_W), BF16),
            jax.ShapeDtypeStruct((m, GROUP), F32),
        ],
        scratch_shapes=[
            pltpu.VMEM((PROJ_TM, d), BF16),
            pltpu.VMEM((PROJ_TM, d), BF16),
            pltpu.VMEM((hpp, d, N_GROUPS * HEAD_DIM), BF16),
        ],
        compiler_params=pltpu.CompilerParams(
            dimension_semantics=("arbitrary", "arbitrary", "arbitrary"),
            vmem_limit_bytes=PROJ_VMEM_LIMIT),
        name="proj",
    )(x2, x2, mod3, mod3, mod3, mod3, norm_g, *([w_in2] * N_GROUPS), q_g, k_g, lower_bounds)


def _build_bias(relb_ref, bias_scr):
    band_w = 4 * CHUNK
    far_w = (LEFT_CHUNKS - 3) * CHUNK
    full = ATT_MASK_VARIANTS - 1
    n_tab = 2 * MAX_REL
    k_i = lax.broadcasted_iota(jnp.int32, (n_tab, band_w), 0)
    c_i = lax.broadcasted_iota(jnp.int32, (n_tab, band_w), 1)
    idx = jnp.clip(3 * CHUNK - c_i, -MAX_REL, MAX_REL) + MAX_REL
    pick = jnp.where(k_i == idx, 1.0, 0.0)
    table = relb_ref[...] * LOG2E
    last = table[:, n_tab:n_tab + 1]
    row0 = jnp.dot(table[:, 0:n_tab], pick, preferred_element_type=F32,
                   precision=lax.Precision.HIGHEST)
    row0 = jnp.where(idx[0:1, :] == n_tab, last, row0)
    r = lax.broadcasted_iota(jnp.int32, (CHUNK, band_w), 0)
    c = lax.broadcasted_iota(jnp.int32, (CHUNK, band_w), 1)
    kpos = lax.broadcasted_iota(jnp.int32, (ATT_RQ, ATT_WK), 1)
    for h in range(N_HEADS):
        far = jnp.broadcast_to(last[h:h + 1, :], (CHUNK, band_w))
        shifted = pltpu.roll(jnp.broadcast_to(row0[h:h + 1, :], (CHUNK, band_w)), 0, 1,
                             stride=1, stride_axis=0)
        near = jnp.where(c >= r, shifted, far)
        for cq in range(ATT_RQ // CHUNK):
            rows = slice(cq * CHUNK, (cq + 1) * CHUNK)
            lo = cq * CHUNK
            bias_scr[full, h, rows, :] = jnp.full((CHUNK, ATT_WK), -jnp.inf, F32)
            bias_scr[full, h, rows, lo:lo + far_w] = jnp.broadcast_to(last[h:h + 1, :],
                                                                      (CHUNK, far_w))
            bias_scr[full, h, rows, lo + far_w:lo + far_w + band_w] = near
        for v in range(full):
            first_valid = LEFT_CHUNKS * CHUNK - ATT_RQ * v
            bias_scr[v, h] = jnp.where(kpos >= first_valid, bias_scr[full, h], -jnp.inf)


def _att_kernel(relb_ref, qz_ref, kv0_ref, kv1_ref, kv2_ref, o_ref, bias_scr):
    b = pl.program_id(0)
    i = pl.program_id(1)

    @pl.when((b == 0) & (i == 0))
    def _():
        _build_bias(relb_ref, bias_scr)

    variant = jnp.minimum(i, ATT_MASK_VARIANTS - 1)
    nt = (((1,), (1,)), ((), ()))
    ones = jnp.ones((ATT_RQ, HEAD_DIM), BF16)
    for h in range(N_HEADS):
        qs = slice(h * QZ_W, h * QZ_W + HEAD_DIM)
        zs = slice(h * QZ_W + HEAD_DIM, (h + 1) * QZ_W)
        q = qz_ref[:, qs]
        s = jnp.concatenate(
            [lax.dot_general(q, kr[:, qs], nt, preferred_element_type=F32)
             for kr in (kv0_ref, kv1_ref, kv2_ref)], axis=1)
        s = s + bias_scr[variant, h]
        m = jnp.max(s, axis=-1, keepdims=True)
        pb = jnp.exp2(s - m).astype(BF16)
        ol = None
        for blk, kr in enumerate((kv0_ref, kv1_ref, kv2_ref)):
            v_aug = jnp.concatenate([kr[:, zs], ones], axis=1)
            part = jnp.dot(pb[:, blk * ATT_RQ:(blk + 1) * ATT_RQ], v_aug,
                           preferred_element_type=F32)
            ol = part if ol is None else ol + part
        o = (ol[:, 0:HEAD_DIM] / ol[:, HEAD_DIM:2 * HEAD_DIM]) * qz_ref[:, zs].astype(F32)
        o_ref[:, _lanes(h)] = o.astype(BF16)


def _att_call(rel_bias, qz, kv, batch, seq_len):
    nq = seq_len // ATT_RQ
    m, w = qz.shape

    def kv_spec(back):
        return pl.BlockSpec((ATT_RQ, w), lambda b, i: (b * nq + jnp.maximum(i - back, 0), 0))
    return pl.pallas_call(
        _att_kernel,
        grid=(batch, nq),
        in_specs=[
            pl.BlockSpec(rel_bias.shape, lambda b, i: (0, 0)),
            pl.BlockSpec((ATT_RQ, w), lambda b, i: (b * nq + i, 0)),
            kv_spec(2), kv_spec(1), kv_spec(0),
        ],
        out_specs=pl.BlockSpec((ATT_RQ, GROUP), lambda b, i: (b * nq + i, 0)),
        out_shape=jax.ShapeDtypeStruct((m, GROUP), BF16),
        scratch_shapes=[pltpu.VMEM((ATT_MASK_VARIANTS, N_HEADS, ATT_RQ, ATT_WK), F32)],
        compiler_params=pltpu.CompilerParams(
            dimension_semantics=("arbitrary", "arbitrary"), vmem_limit_bytes=VMEM_LIMIT),
        name="att",
    )(rel_bias, qz, kv, kv, kv)


def _hg_piece(h, k):
    return slice(h * HGP_W + k * HEAD_DIM, h * HGP_W + (k + 1) * HEAD_DIM)


def _per_chunk_rows(rows, width):
    return jnp.concatenate([jnp.broadcast_to(r, (CHUNK, width)) for r in rows], axis=0)


def _hg_kernel(p_ref, logf_ref, og_ref, o_ref,
               st_scr, s0_scr, qf_scr, kf_scr, vf_scr, bf_scr, acc_scr):
    t = pl.program_id(1)

    @pl.when(t == 0)
    def _():
        st_scr[...] = jnp.zeros_like(st_scr)

    rb = HG_RB
    c = CHUNK
    row_i = lax.broadcasted_iota(jnp.int32, (rb, rb), 0)
    col_i = lax.broadcasted_iota(jnp.int32, (rb, rb), 1)
    tril = row_i >= col_i
    tri_bf = jnp.where(tril, 1.0, 0.0).astype(BF16)
    same_chunk_tril = tril & ((row_i // c) == (col_i // c))
    nt = (((1,), (1,)), ((), ()))
    tn = (((0,), (0,)), ((), ()))
    og = og_ref[...]

    lf = logf_ref[...]
    hi = lf.astype(BF16)
    r1 = lf - hi.astype(F32)
    md = r1.astype(BF16)
    lo = (r1 - md.astype(F32)).astype(BF16)
    beta = (jnp.dot(tri_bf, lo, preferred_element_type=F32)
            + jnp.dot(tri_bf, md, preferred_element_type=F32)
            + jnp.dot(tri_bf, hi, preferred_element_type=F32))

    mids = [beta[k * c + c // 2 - 1:k * c + c // 2, :] for k in range(4)]
    ends = [beta[k * c + c - 1:k * c + c, :] for k in range(4)]
    x0 = beta - _per_chunk_rows(mids, GROUP)
    e1 = jnp.exp(-jnp.abs(beta - _per_chunk_rows([ends[0], ends[0], ends[2], ends[2]], GROUP)))
    e2 = jnp.exp(-jnp.abs(beta - ends[1]))
    eq = jnp.exp(beta)
    ek = jnp.exp(ends[3] - beta)
    g_blk = jnp.exp(ends[3])
    edge = jnp.concatenate([x0[k * c:k * c + 1, :] for k in range(4)]
                           + [x0[k * c + c - 1:k * c + c, :] for k in range(4)], axis=0)
    steep = jnp.max(jnp.abs(edge)) > EXP_CLAMP

    z64 = jnp.zeros((c, HEAD_DIM), BF16)

    def cross_scores(qt, kh, hs):
        def scaled(v, e, lo_row, hi_row):
            return (v[lo_row:hi_row] * e[lo_row:hi_row, hs]).astype(BF16)
        lhs = jnp.concatenate([
            jnp.concatenate([z64, scaled(qt, e1, c, 2 * c), z64, z64], axis=0),
            jnp.concatenate([z64, z64, z64, scaled(qt, e1, 3 * c, 4 * c)], axis=0),
            jnp.concatenate([z64, z64, scaled(qt, e2, 2 * c, 4 * c)], axis=0)], axis=1)
        rhs = jnp.concatenate([
            jnp.concatenate([scaled(kh, e1, 0, c), z64, z64, z64], axis=0),
            jnp.concatenate([z64, z64, scaled(kh, e1, 2 * c, 3 * c), z64], axis=0),
            jnp.concatenate([scaled(kh, e2, 0, 2 * c), z64, z64], axis=0)], axis=1)
        return lax.dot_general(lhs, rhs, nt, preferred_element_type=F32)

    def carried(qt, st, hs):
        qb = (qt * eq[:, hs]).astype(BF16)
        return lax.dot_general(qb, st.astype(BF16), nt, preferred_element_type=F32)

    def write_out(h, o):
        ms = jnp.mean(o * o, axis=-1, keepdims=True)
        y = o * lax.rsqrt(ms + EPS) * og
        o_ref[:, _lanes(h)] = (y * p_ref[:, _hg_piece(h, 3)].astype(F32)).astype(BF16)

    for h in range(N_HEADS):
        hs = _lanes(h)
        qt = p_ref[:, _hg_piece(h, 0)].astype(F32)
        kh = p_ref[:, _hg_piece(h, 1)].astype(F32)
        vh = p_ref[:, _hg_piece(h, 2)]
        st = st_scr[h]
        s0_scr[h] = st
        xh = x0[:, hs]
        q0 = (qt * jnp.exp(jnp.minimum(xh, EXP_CLAMP))).astype(BF16)
        k0 = (kh * jnp.exp(jnp.minimum(-xh, EXP_CLAMP))).astype(BF16)
        a0 = lax.dot_general(q0, k0, nt, preferred_element_type=F32)
        a = jnp.where(same_chunk_tril, a0, cross_scores(qt, kh, hs)).astype(BF16)
        o = jnp.dot(a, vh, preferred_element_type=F32) + carried(qt, st, hs)
        kd = (kh * ek[:, hs]).astype(BF16)
        st_scr[h] = st * g_blk[:, hs] + lax.dot_general(vh, kd, tn, preferred_element_type=F32)
        write_out(h, o)

    @pl.when(steep)
    def _():
        for h in range(N_HEADS):
            qf_scr[:, _lanes(h)] = p_ref[:, _hg_piece(h, 0)].astype(F32)
            kf_scr[:, _lanes(h)] = p_ref[:, _hg_piece(h, 1)].astype(F32)
            vf_scr[:, _lanes(h)] = p_ref[:, _hg_piece(h, 2)].astype(F32)
        bf_scr[...] = beta
        acc_scr[...] = jnp.zeros_like(acc_scr)
        t_idx = lax.broadcasted_iota(jnp.int32, (c, HEAD_DIM), 0)

        def pair_body(s, carry):
            base = pl.multiple_of((s // c) * c, c)
            rows = pl.ds(base, c)
            k_all = kf_scr[pl.ds(s, 1), :]
            v_all = vf_scr[pl.ds(s, 1), :]
            b_all = bf_scr[pl.ds(s, 1), :]
            for h in range(N_HEADS):
                hs = _lanes(h)
                w = jnp.exp(jnp.minimum(bf_scr[rows, hs] - b_all[:, hs], 0.0))
                p = jnp.where(t_idx >= s - base, qf_scr[rows, hs] * k_all[:, hs] * w, 0.0)
                acc_scr[rows, hs] += jnp.sum(p, axis=-1, keepdims=True) * v_all[:, hs]
            return carry
        lax.fori_loop(0, rb, pair_body, 0)

        for h in range(N_HEADS):
            hs = _lanes(h)
            qt = p_ref[:, _hg_piece(h, 0)].astype(F32)
            kh = p_ref[:, _hg_piece(h, 1)].astype(F32)
            vh = p_ref[:, _hg_piece(h, 2)]
            a = cross_scores(qt, kh, hs).astype(BF16)
            o = (jnp.dot(a, vh, preferred_element_type=F32) + carried(qt, s0_scr[h], hs)
                 + acc_scr[:, hs])
            write_out(h, o)


def _hg_call(hgp, logf, o_g, batch, seq_len):
    nt_ = seq_len // HG_RB
    m, w = hgp.shape
    return pl.pallas_call(
        _hg_kernel,
        grid=(batch, nt_),
        in_specs=[
            pl.BlockSpec((HG_RB, w), lambda b, t: (b * nt_ + t, 0)),
            pl.BlockSpec((HG_RB, GROUP), lambda b, t: (b * nt_ + t, 0)),
            pl.BlockSpec((1, HEAD_DIM), lambda b, t: (0, 0)),
        ],
        out_specs=pl.BlockSpec((HG_RB, GROUP), lambda b, t: (b * nt_ + t, 0)),
        out_shape=jax.ShapeDtypeStruct((m, GROUP), BF16),
        scratch_shapes=[
            pltpu.VMEM((N_HEADS, HEAD_DIM, HEAD_DIM), F32),
            pltpu.VMEM((N_HEADS, HEAD_DIM, HEAD_DIM), F32),
        ] + [pltpu.VMEM((HG_RB, GROUP), F32)] * 5,
        compiler_params=pltpu.CompilerParams(
            dimension_semantics=("arbitrary", "arbitrary"), vmem_limit_bytes=VMEM_LIMIT),
        name="hgrn",
    )(hgp, logf, o_g)


def _out_kernel(att_ref, hg_ref, w_ref, x_ref, gate_ref, o_ref, w_scr):
    @pl.when(pl.program_id(0) == 0)
    def _():
        w_scr[...] = w_ref[...].astype(BF16)

    y = jnp.dot(att_ref[...], w_scr[0:GROUP, :], preferred_element_type=F32)
    y += jnp.dot(hg_ref[...], w_scr[GROUP:2 * GROUP, :], preferred_element_type=F32)
    o_ref[...] = x_ref[...] + gate_ref[...] * y


def _out_call(att, hg, w_out2, x2, mod3, seq_len):
    m, d = x2.shape
    tiles_per_batch = seq_len // OUT_TM
    return pl.pallas_call(
        _out_kernel,
        grid=(m // OUT_TM,),
        in_specs=[
            pl.BlockSpec((OUT_TM, GROUP), lambda i: (i, 0)),
            pl.BlockSpec((OUT_TM, GROUP), lambda i: (i, 0)),
            pl.BlockSpec(w_out2.shape, lambda i: (0, 0), pipeline_mode=pl.Buffered(1)),
            pl.BlockSpec((OUT_TM, d), lambda i: (i, 0)),
            pl.BlockSpec((None, 1, d), lambda i: (i // tiles_per_batch, 0, 2)),
        ],
        out_specs=pl.BlockSpec((OUT_TM, d), lambda i: (i, 0)),
        out_shape=jax.ShapeDtypeStruct((m, d), F32),
        scratch_shapes=[pltpu.VMEM(w_out2.shape, BF16)],
        compiler_params=pltpu.CompilerParams(
            dimension_semantics=("arbitrary",), vmem_limit_bytes=VMEM_LIMIT),
        name="outproj",
    )(att, hg, w_out2, x2, mod3)


def kernel(x, c, norm_g, w_ada, b_ada, w_in, q_norm_g, k_norm_g, rel_bias,
           lower_bounds, hg_norm_g, w_out):
    batch, seq_len, d = x.shape
    assert w_in.shape == (1, d, N_GROUPS * GROUP) and w_out.shape == (1, 2 * GROUP, d)
    assert seq_len % PROJ_TM == 0 and seq_len % ATT_RQ == 0 and seq_len % HG_RB == 0
    assert rel_bias.shape == (1, N_HEADS, 2 * MAX_REL + 1)

    x2 = x.reshape(batch * seq_len, d)
    c_pad = jnp.pad(c, ((0, 8 - batch), (0, 0)))
    mod = _ada_call(c_pad, w_ada, b_ada, batch)
    mod3 = mod[:batch].reshape(batch, 1, 3 * d)

    qz, kv, hgp, logf = _proj_call(x2, mod3, norm_g, w_in[0], q_norm_g, k_norm_g,
                                   lower_bounds, seq_len)
    att = _att_call(rel_bias[0], qz, kv, batch, seq_len)
    hg = _hg_call(hgp, logf, hg_norm_g, batch, seq_len)
    out = _out_call(att, hg, w_out[0], x2, mod3, seq_len)
    return out.reshape(batch, seq_len, d)
```
